```python
import math
import jax, jax.numpy as jnp
from jax import lax
import numpy as np

D_MODEL = 1024
BATCH = 32
SEQ = 2048
DEPTH = 2
DEC_BATCH = 16
DEC_SEQ = 2048
PAST_LEN = 128

RET_HEADS = 8
RET_QK_DIM = 64
RET_V_DIM = 128
RET_QK = RET_HEADS * RET_QK_DIM
RET_V = RET_HEADS * RET_V_DIM
RET_CHUNK = 128
ROPE_BASE = 10000.0
SSM_WIDTH = 1024
SSM_GROUP = 16
SSM_GROUPS = SSM_WIDTH // SSM_GROUP
SSM_STATE = 64
DT_MIN = 1e-3
DT_MAX = 1e-1
EIG_CLIP = -1e-4
D_FF = 2816
ALPHA = (2 * DEPTH) ** 0.25
BETA = (8 * DEPTH) ** -0.25
LN_EPS = 1e-5
IN_SPLITS = (RET_QK, 2 * RET_QK, 2 * RET_QK + RET_V, 2 * RET_QK + 2 * RET_V,
             2 * RET_QK + 2 * RET_V + SSM_WIDTH, 2 * RET_QK + 2 * RET_V + SSM_WIDTH + D_MODEL)
IN_COLS = 2 * RET_QK + 2 * RET_V + SSM_WIDTH + 2 * D_MODEL

kernel_name = "hybrid_retention_s5_macaron_encoder"

F32 = jnp.float32


def layer_norm(x, g, b):
    xf = x.astype(F32)
    mu = jnp.mean(xf, -1, keepdims=True)
    var = jnp.mean(jnp.square(xf - mu), -1, keepdims=True)
    return ((xf - mu) * lax.rsqrt(var + LN_EPS) * g.astype(F32) + b.astype(F32)).astype(x.dtype)


def swiglu_ffn(x, w_gu, w_down):
    a, u = jnp.split(x @ w_gu, 2, axis=-1)
    return (jax.nn.silu(a) * u) @ w_down


def rotate(t, pos):
    half = t.shape[-1] // 2
    inv_freq = ROPE_BASE ** (-jnp.arange(half, dtype=F32) / half)
    ang = pos[:, None] * inv_freq[None, :]
    cos = jnp.cos(ang)[None, :, None, :]
    sin = jnp.sin(ang)[None, :, None, :]
    t1, t2 = t[..., :half], t[..., half:]
    return jnp.concatenate([t1 * cos - t2 * sin, t1 * sin + t2 * cos], axis=-1)


def retention_direction(q, k, v, log_gamma, strict):
    idx = jnp.arange(RET_CHUNK, dtype=F32)
    dist = idx[:, None] - idx[None, :]
    mask = (dist > 0) if strict else (dist >= 0)
    decay = jnp.where(mask[None], jnp.exp(log_gamma[:, None, None] * jnp.where(mask, dist, 0.0)[None]), 0.0)
    scores = jnp.einsum('bnihd,bnjhd->bnhij', q, k) * decay[None, None]
    inner = jnp.einsum('bnhij,bnjhe->bnihe', scores, v)
    xi = jnp.exp((idx[:, None] + 1.0) * log_gamma[None, :])
    zeta = jnp.exp((RET_CHUNK - 1.0 - idx[:, None]) * log_gamma[None, :])
    chunk_decay = jnp.exp(RET_CHUNK * log_gamma)[None, :, None, None]
    kv = jnp.einsum('bnjhd,bnjhe->nbhde', k * zeta[:, :, None], v)

    def step(state, kv_n):
        return chunk_decay * state + kv_n, state

    _, prev = lax.scan(step, jnp.zeros_like(kv[0]), kv)
    cross = jnp.einsum('bnihd,nbhde->bnihe', q * xi[:, :, None], prev)
    return inner + cross


def retention_branch(q, k, v, g, w_o):
    B, L, _ = q.shape
    nc = L // RET_CHUNK
    pos = jnp.arange(L, dtype=F32)
    qh = rotate(q.astype(F32).reshape(B, L, RET_HEADS, RET_QK_DIM), pos)
    kh = rotate(k.astype(F32).reshape(B, L, RET_HEADS, RET_QK_DIM), pos) * (RET_QK_DIM ** -0.5)
    vh = v.astype(F32).reshape(B, L, RET_HEADS, RET_V_DIM)
    log_gamma = jnp.log1p(-jnp.exp2(-5.0 - jnp.arange(RET_HEADS, dtype=F32)))
    to_chunks = lambda t: t.reshape(B, nc, RET_CHUNK, *t.shape[2:])
    rev = lambda t: jnp.flip(t, axis=1)
    o_fwd = retention_direction(to_chunks(qh), to_chunks(kh), to_chunks(vh), log_gamma, False)
    o_bwd = retention_direction(to_chunks(rev(qh)), to_chunks(rev(kh)), to_chunks(rev(vh)), log_gamma, True)
    o = o_fwd.reshape(B, L, RET_HEADS, RET_V_DIM) + rev(o_bwd.reshape(B, L, RET_HEADS, RET_V_DIM))
    mu = jnp.mean(o, -1, keepdims=True)
    var = jnp.mean(jnp.square(o - mu), -1, keepdims=True)
    o = (o - mu) * lax.rsqrt(var + LN_EPS)
    out = jax.nn.silu(g.astype(F32)) * o.reshape(B, L, RET_V)
    return out.astype(q.dtype) @ w_o


def _complex_affine_combine(e1, e2):
    a1r, a1i, b1r, b1i = e1
    a2r, a2i, b2r, b2i = e2
    return (a1r * a2r - a1i * a2i,
            a1r * a2i + a1i * a2r,
            a2r * b1r - a2i * b1i + b2r,
            a2r * b1i + a2i * b1r + b2i)


def s5_direction(us, lam_re, lam_im, log_dt, b_re, b_im, c_re, c_im, reverse):
    L = us.shape[0]
    lr = jnp.minimum(lam_re.astype(F32), EIG_CLIP)
    li = lam_im.astype(F32)
    dt = jnp.exp(log_dt.astype(F32))[:, None]
    mag = jnp.exp(lr * dt)
    ab_re = mag * jnp.cos(li * dt)
    ab_im = mag * jnp.sin(li * dt)
    den = lr * lr + li * li
    num_re = ab_re - 1.0
    f_re = (num_re * lr + ab_im * li) / den
    f_im = (ab_im * lr - num_re * li) / den
    br, bi = b_re.astype(F32), b_im.astype(F32)
    bb_re = f_re[..., None] * br - f_im[..., None] * bi
    bb_im = f_re[..., None] * bi + f_im[..., None] * br
    bu_re = jnp.einsum('lbgp,gnp->lbgn', us, bb_re)
    bu_im = jnp.einsum('lbgp,gnp->lbgn', us, bb_im)
    a_re = jnp.broadcast_to(ab_re[None, None], (L, 1) + ab_re.shape)
    a_im = jnp.broadcast_to(ab_im[None, None], (L, 1) + ab_im.shape)
    _, _, h_re, h_im = lax.associative_scan(_complex_affine_combine, (a_re, a_im, bu_re, bu_im),
                                            reverse=reverse, axis=0)
    return (jnp.einsum('gpn,lbgn->lbgp', c_re.astype(F32), h_re)
            - jnp.einsum('gpn,lbgn->lbgp', c_im.astype(F32), h_im))


def s5_branch(u, lam_re, lam_im, log_dt, b_re, b_im, c_re, c_im, d_skip, w_glu):
    B, L, _ = u.shape
    us = jnp.swapaxes(u.astype(F32), 0, 1).reshape(L, B, SSM_GROUPS, SSM_GROUP)
    y_f = s5_direction(us, lam_re[0], lam_im[0], log_dt[0], b_re[0], b_im[0], c_re[0], c_im[0], False)
    y_b = s5_direction(us, lam_re[1], lam_im[1], log_dt[1], b_re[1], b_im[1], c_re[1], c_im[1], True)
    y = (y_f + y_b + us * d_skip.astype(F32).reshape(SSM_GROUPS, SSM_GROUP)).reshape(L, B, SSM_WIDTH)
    z = jax.nn.gelu(jnp.swapaxes(y, 0, 1)).astype(u.dtype)
    val, gate = jnp.split(z @ w_glu, 2, axis=-1)
    return val * jax.nn.sigmoid(gate)


def hybrid_mixer(x, w_in, b_gate, ret_w_o, s5_lam_re, s5_lam_im, s5_log_dt, s5_b_re, s5_b_im,
                 s5_c_re, s5_c_im, s5_d, s5_w_glu, w_out):
    h = x @ w_in
    q, k, v, g, u, gate_ret, gate_ssm = jnp.split(h, IN_SPLITS, axis=-1)
    y_ret = retention_branch(q, k, v, g, ret_w_o)
    y_ssm = s5_branch(u, s5_lam_re, s5_lam_im, s5_log_dt, s5_b_re, s5_b_im, s5_c_re, s5_c_im, s5_d, s5_w_glu)
    bg_ret, bg_ssm = jnp.split(b_gate.astype(F32), 2)
    g_ret = jax.nn.sigmoid(gate_ret.astype(F32) + bg_ret)
    g_ssm = jax.nn.sigmoid(gate_ssm.astype(F32) + bg_ssm)
    merged = g_ret * y_ret.astype(F32) + g_ssm * y_ssm.astype(F32)
    return merged.astype(x.dtype) @ w_out


def trunk(x, params):
    (ffn1_w_gu, ffn1_w_down, ln1_g, ln1_b, w_in, b_gate, ret_w_o, s5_lam_re, s5_lam_im, s5_log_dt,
     s5_b_re, s5_b_im, s5_c_re, s5_c_im, s5_d, s5_w_glu, w_out, ln2_g, ln2_b,
     ffn2_w_gu, ffn2_w_down, ln3_g, ln3_b) = params
    for l in range(DEPTH):
        x = layer_norm(ALPHA * x + 0.5 * swiglu_ffn(x, ffn1_w_gu[l], ffn1_w_down[l]), ln1_g[l], ln1_b[l])
        mix = hybrid_mixer(x, w_in[l], b_gate[l], ret_w_o[l], s5_lam_re[l], s5_lam_im[l], s5_log_dt[l],
                           s5_b_re[l], s5_b_im[l], s5_c_re[l], s5_c_im[l], s5_d[l], s5_w_glu[l], w_out[l])
        x = layer_norm(ALPHA * x + mix, ln2_g[l], ln2_b[l])
        x = layer_norm(ALPHA * x + 0.5 * swiglu_ffn(x, ffn2_w_gu[l], ffn2_w_down[l]), ln3_g[l], ln3_b[l])
    return x


def setup_inputs(seed: int = 0) -> dict:
    key = jax.random.key(seed)
    ks = jax.random.split(key, 28)
    nrm = lambda k, shape, scale: jax.random.normal(k, shape, F32) * scale
    gain = lambda k: 1.0 + 0.02 * jax.random.normal(k, (DEPTH, D_MODEL), F32)
    bias = lambda k: 0.02 * jax.random.normal(k, (DEPTH, D_MODEL), F32)
    n_idx = jnp.arange(SSM_STATE, dtype=F32)
    lam_shape = (DEPTH, 2, SSM_GROUPS, SSM_STATE)
    return {
        "x_prompt": jax.random.normal(ks[0], (BATCH, SEQ, D_MODEL), F32),
        "x_sample": jax.random.normal(ks[1], (DEC_BATCH, DEC_SEQ, D_MODEL), F32),
        "ffn1_w_gu": nrm(ks[2], (DEPTH, D_MODEL, 2 * D_FF), D_MODEL ** -0.5),
        "ffn1_w_down": nrm(ks[3], (DEPTH, D_FF, D_MODEL), BETA * D_FF ** -0.5),
        "ln1_g": gain(ks[4]),
        "ln1_b": bias(ks[5]),
        "w_in": nrm(ks[6], (DEPTH, D_MODEL, IN_COLS), D_MODEL ** -0.5),
        "b_gate": nrm(ks[7], (DEPTH, 2 * D_MODEL), 0.01),
        "ret_w_o": nrm(ks[8], (DEPTH, RET_V, D_MODEL), RET_V ** -0.5),
        "s5_lam_re": -0.5 + 0.01 * jax.random.normal(ks[9], lam_shape, F32),
        "s5_lam_im": math.pi * n_idx + 0.01 * jax.random.normal(ks[10], lam_shape, F32),
        "s5_log_dt": jax.random.uniform(ks[11], (DEPTH, 2, SSM_GROUPS), F32,
                                        math.log(DT_MIN), math.log(DT_MAX)),
        "s5_b_re": nrm(ks[12], (DEPTH, 2, SSM_GROUPS, SSM_STATE, SSM_GROUP), (2 * SSM_GROUP) ** -0.5),
        "s5_b_im": nrm(ks[13], (DEPTH, 2, SSM_GROUPS, SSM_STATE, SSM_GROUP), (2 * SSM_GROUP) ** -0.5),
        "s5_c_re": nrm(ks[14], (DEPTH, 2, SSM_GROUPS, SSM_GROUP, SSM_STATE), (2 * SSM_STATE) ** -0.5),
        "s5_c_im": nrm(ks[15], (DEPTH, 2, SSM_GROUPS, SSM_GROUP, SSM_STATE), (2 * SSM_STATE) ** -0.5),
        "s5_d": nrm(ks[16], (DEPTH, SSM_WIDTH), 1.0),
        "s5_w_glu": nrm(ks[17], (DEPTH, SSM_WIDTH, 2 * D_MODEL), SSM_WIDTH ** -0.5),
        "w_out": nrm(ks[18], (DEPTH, D_MODEL, D_MODEL), BETA * D_MODEL ** -0.5),
        "ln2_g": gain(ks[19]),
        "ln2_b": bias(ks[20]),
        "ffn2_w_gu": nrm(ks[21], (DEPTH, D_MODEL, 2 * D_FF), D_MODEL ** -0.5),
        "ffn2_w_down": nrm(ks[22], (DEPTH, D_FF, D_MODEL), BETA * D_FF ** -0.5),
        "ln3_g": gain(ks[23]),
        "ln3_b": bias(ks[24]),
    }


def reference(x_prompt, x_sample, ffn1_w_gu, ffn1_w_down, ln1_g, ln1_b, w_in, b_gate, ret_w_o,
              s5_lam_re, s5_lam_im, s5_log_dt, s5_b_re, s5_b_im, s5_c_re, s5_c_im, s5_d, s5_w_glu,
              w_out, ln2_g, ln2_b, ffn2_w_gu, ffn2_w_down, ln3_g, ln3_b):
    params = (ffn1_w_gu, ffn1_w_down, ln1_g, ln1_b, w_in, b_gate, ret_w_o, s5_lam_re, s5_lam_im, s5_log_dt,
              s5_b_re, s5_b_im, s5_c_re, s5_c_im, s5_d, s5_w_glu, w_out, ln2_g, ln2_b,
              ffn2_w_gu, ffn2_w_down, ln3_g, ln3_b)
    y_prompt = trunk(x_prompt, params)
    y_sample = trunk(x_sample, params)
    return (y_prompt, y_sample)
```

```python
import functools
import math

import jax
import jax.numpy as jnp
from jax import lax
from jax.experimental import pallas as pl
from jax.experimental.pallas import tpu as pltpu

F32 = jnp.float32
BF16 = jnp.bfloat16

D_MODEL = 1024
DEPTH = 2
RET_HEADS = 8
RET_QK_DIM = 64
RET_V_DIM = 128
RET_QK = RET_HEADS * RET_QK_DIM
RET_V = RET_HEADS * RET_V_DIM
ROPE_BASE = 10000.0
SSM_WIDTH = 1024
SSM_GROUP = 16
SSM_GROUPS = SSM_WIDTH // SSM_GROUP
SSM_STATE = 64
EIG_CLIP = -1e-4
D_FF = 2816
ALPHA = (2 * DEPTH) ** 0.25
LN_EPS = 1e-5

LANES = 128
MXU_DIM = 256
VMEM_LIMIT_BYTES = 56 * 1024 * 1024

TOKEN_TILE = 512
FF_CHUNK = 704
RET_CHUNK = MXU_DIM
S5_CHUNK = MXU_DIM // SSM_GROUP
S5_BATCH_TILE = 16


def _resident(shape):
    nd = len(shape)
    return pl.BlockSpec(shape, lambda *_: (0,) * nd, pipeline_mode=pl.Buffered(1))


def _params(*sem):
    return pltpu.CompilerParams(dimension_semantics=sem, vmem_limit_bytes=VMEM_LIMIT_BYTES)


def _dot(a, b):
    return jnp.dot(a, b, preferred_element_type=F32)


def _sigmoid(x):
    return 1.0 / (1.0 + jnp.exp(-x))


def _layer_norm(y, g, b):
    mu = jnp.mean(y, axis=-1, keepdims=True)
    yc = y - mu
    var = jnp.mean(yc * yc, axis=-1, keepdims=True)
    return yc * lax.rsqrt(var + LN_EPS) * g + b


def _ffn_ln_kernel(x_ref, wgu_ref, wdn_ref, g_ref, b_ref, o_ref):
    x = x_ref[...]
    xb = x.astype(BF16)
    acc = jnp.zeros(x.shape, F32)
    for c in range(D_FF // FF_CHUNK):
        lo = c * FF_CHUNK
        a = _dot(xb, wgu_ref[:, lo:lo + FF_CHUNK])
        u = _dot(xb, wgu_ref[:, D_FF + lo:D_FF + lo + FF_CHUNK])
        h = (a * _sigmoid(a) * u).astype(BF16)
        acc = acc + _dot(h, wdn_ref[lo:lo + FF_CHUNK, :])
    o_ref[...] = _layer_norm(ALPHA * x + 0.5 * acc, g_ref[...], b_ref[...])


def _ffn_ln(x, w_gu, w_down, ln_g, ln_b):
    t = x.shape[0]
    tile = pl.BlockSpec((TOKEN_TILE, D_MODEL), lambda i: (i, 0))
    return pl.pallas_call(
        _ffn_ln_kernel,
        grid=(t // TOKEN_TILE,),
        in_specs=[tile, _resident(w_gu.shape), _resident(w_down.shape),
                  _resident((1, D_MODEL)), _resident((1, D_MODEL))],
        out_specs=tile,
        out_shape=jax.ShapeDtypeStruct((t, D_MODEL), F32),
        compiler_params=_params("parallel"),
        name="ffn_ln",
    )(x, w_gu, w_down, ln_g.reshape(1, D_MODEL), ln_b.reshape(1, D_MODEL))


def _in_proj_kernel(x_ref, w_ref, bg_ref, cos_ref, sin_ref,
                    q_ref, k_ref, v_ref, g_ref, u_ref, gr_ref, gs_ref):
    xb = x_ref[...].astype(BF16)

    def seg(lo, width):
        return _dot(xb, w_ref[:, lo:lo + width])

    lane = lax.broadcasted_iota(jnp.int32, (1, RET_QK), 1)
    first_half = (lane % RET_QK_DIM) < (RET_QK_DIM // 2)
    cos = cos_ref[...]
    sin = sin_ref[...]

    def rotary(t):
        half = RET_QK_DIM // 2
        swapped = jnp.where(first_half, pltpu.roll(t, RET_QK - half, axis=1), pltpu.roll(t, half, axis=1))
        return t * cos + swapped * sin

    q_ref[...] = rotary(seg(0, RET_QK)).astype(BF16)
    k_ref[...] = (rotary(seg(RET_QK, RET_QK)) * (RET_QK_DIM ** -0.5)).astype(BF16)
    base = 2 * RET_QK
    v_ref[...] = seg(base, RET_V).astype(BF16)
    g_ref[...] = seg(base + RET_V, RET_V).astype(BF16)
    u_ref[...] = seg(base + 2 * RET_V, SSM_WIDTH).astype(BF16)
    gate0 = base + 2 * RET_V + SSM_WIDTH
    gr_ref[...] = _sigmoid(seg(gate0, D_MODEL) + bg_ref[:, :D_MODEL]).astype(BF16)
    gs_ref[...] = _sigmoid(seg(gate0 + D_MODEL, D_MODEL) + bg_ref[:, D_MODEL:]).astype(BF16)


def _in_proj(x, w_in, b_gate, cos_t, sin_t, seq_len):
    t = x.shape[0]
    tiles_per_seq = seq_len // TOKEN_TILE
    tile = lambda w: pl.BlockSpec((TOKEN_TILE, w), lambda i: (i, 0))
    table = pl.BlockSpec((TOKEN_TILE, RET_QK), lambda i: (i % tiles_per_seq, 0))
    widths = (RET_QK, RET_QK, RET_V, RET_V, SSM_WIDTH, D_MODEL, D_MODEL)
    return pl.pallas_call(
        _in_proj_kernel,
        grid=(t // TOKEN_TILE,),
        in_specs=[tile(D_MODEL), _resident(w_in.shape), _resident((1, 2 * D_MODEL)), table, table],
        out_specs=[tile(w) for w in widths],
        out_shape=[jax.ShapeDtypeStruct((t, w), BF16) for w in widths],
        compiler_params=_params("parallel"),
        name="in_proj",
    )(x, w_in, b_gate.reshape(1, 2 * D_MODEL), cos_t, sin_t)


def _rotary_tables(seq_len):
    half = RET_QK_DIM // 2
    inv_freq = ROPE_BASE ** (-jnp.arange(half, dtype=F32) / half)
    ang = jnp.arange(seq_len, dtype=F32)[:, None] * inv_freq[None, :]
    cos = jnp.cos(ang)
    sin = jnp.sin(ang)
    cos_t = jnp.tile(jnp.concatenate([cos, cos], axis=-1), (1, RET_HEADS))
    sin_t = jnp.tile(jnp.concatenate([-sin, sin], axis=-1), (1, RET_HEADS))
    return cos_t, sin_t


def _retention_tables():
    c = RET_CHUNK
    log_gamma = jnp.log1p(-jnp.exp2(-5.0 - jnp.arange(RET_HEADS, dtype=F32)))
    idx = jnp.arange(c, dtype=F32)
    dist = jnp.abs(idx[:, None] - idx[None, :])
    decay = jnp.exp(log_gamma[:, None, None] * dist[None])
    expo = jnp.stack([idx + 1.0, c - idx, c - 1.0 - idx, idx], axis=0)
    edge = jnp.exp(log_gamma[:, None, None] * expo[None])
    edge = jnp.broadcast_to(edge[..., None], (RET_HEADS, 4, c, LANES))
    return decay, edge


def _retention_kernel(q_ref, k_ref, v_ref, g_ref, decay_ref, edge_ref, o_ref, kvf_ref, kvb_ref):
    c = RET_CHUNK
    nc = q_ref.shape[1] // c
    lane = lax.broadcasted_iota(jnp.int32, (1, LANES), 1)
    contract_rows = (((0,), (0,)), ((), ()))
    contract_lanes = (((1,), (1,)), ((), ()))
    for e in range(2):
        mine = (lane < RET_QK_DIM) if e == 0 else (lane >= RET_QK_DIM)
        vsl = slice(e * RET_V_DIM, (e + 1) * RET_V_DIM)
        xi_f, xi_b = edge_ref[e, 0], edge_ref[e, 1]
        zeta_f, zeta_b = edge_ref[e, 2], edge_ref[e, 3]
        chunk_decay = xi_f[c - 1:c, :]

        def k_masked(n):
            kc = k_ref[0, n * c:(n + 1) * c, :]
            return jnp.where(mine, kc, jnp.zeros_like(kc))

        state = jnp.zeros((LANES, RET_V_DIM), F32)
        for n in range(nc):
            kvf_ref[n] = state
            if n + 1 < nc:
                kz = (k_masked(n).astype(F32) * zeta_f).astype(BF16)
                kv = lax.dot_general(kz, v_ref[0, n * c:(n + 1) * c, vsl], contract_rows,
                                     preferred_element_type=F32)
                state = chunk_decay * state + kv
        state = jnp.zeros((LANES, RET_V_DIM), F32)
        for n in range(nc - 1, -1, -1):
            kvb_ref[n] = state
            if n > 0:
                kz = (k_masked(n).astype(F32) * zeta_b).astype(BF16)
                kv = lax.dot_general(kz, v_ref[0, n * c:(n + 1) * c, vsl], contract_rows,
                                     preferred_element_type=F32)
                state = chunk_decay * state + kv

        for n in range(nc):
            rows = slice(n * c, (n + 1) * c)
            qc = q_ref[0, rows, :]
            vc = v_ref[0, rows, vsl]
            scores = lax.dot_general(qc, k_masked(n), contract_lanes, preferred_element_type=F32)
            scores = scores * decay_ref[e]
            o = _dot(scores.astype(BF16), vc)
            qf = qc.astype(F32)
            o = o + _dot((qf * xi_f).astype(BF16), kvf_ref[n].astype(BF16))
            o = o + _dot((qf * xi_b).astype(BF16), kvb_ref[n].astype(BF16))
            mu = jnp.mean(o, axis=-1, keepdims=True)
            oc = o - mu
            var = jnp.mean(oc * oc, axis=-1, keepdims=True)
            on = oc * lax.rsqrt(var + LN_EPS)
            gate = g_ref[0, rows, vsl].astype(F32)
            o_ref[0, rows, vsl] = (gate * _sigmoid(gate) * on).astype(BF16)


def _retention(q, k, v, g, decay, edge):
    b, seq_len, _ = q.shape
    nc = seq_len // RET_CHUNK
    qk_spec = pl.BlockSpec((1, seq_len, LANES), lambda i, h: (i, 0, h))
    v_spec = pl.BlockSpec((1, seq_len, 2 * RET_V_DIM), lambda i, h: (i, 0, h))
    return pl.pallas_call(
        _retention_kernel,
        grid=(b, RET_HEADS // 2),
        in_specs=[qk_spec, qk_spec, v_spec, v_spec,
                  pl.BlockSpec((2, RET_CHUNK, RET_CHUNK), lambda i, h: (h, 0, 0)),
                  pl.BlockSpec((2, 4, RET_CHUNK, LANES), lambda i, h: (h, 0, 0, 0))],
        out_specs=v_spec,
        out_shape=jax.ShapeDtypeStruct((b, seq_len, RET_V), BF16),
        scratch_shapes=[pltpu.VMEM((nc, LANES, RET_V_DIM), F32), pltpu.VMEM((nc, LANES, RET_V_DIM), F32)],
        compiler_params=_params("parallel", "parallel"),
        name="retention",
    )(q, k, v, g, decay, edge)


def _s5_matrices(lam_re, lam_im, log_dt, b_re, b_im, c_re, c_im, d_skip):
    hi = lax.Precision.HIGHEST
    cs, n, p = S5_CHUNK, SSM_STATE, SSM_GROUP
    lr = jnp.minimum(lam_re.astype(F32), EIG_CLIP)
    li = lam_im.astype(F32)
    dt = jnp.exp(log_dt.astype(F32))[..., None]
    steps = jnp.arange(cs + 1, dtype=F32)[:, None, None, None]
    mag = jnp.exp(lr * dt * steps)
    pw_re = mag * jnp.cos(li * dt * steps)
    pw_im = mag * jnp.sin(li * dt * steps)
    ab_re, ab_im = pw_re[1], pw_im[1]
    den = lr * lr + li * li
    num_re = ab_re - 1.0
    f_re = (num_re * lr + ab_im * li) / den
    f_im = (ab_im * lr - num_re * li) / den
    br, bi = b_re.astype(F32), b_im.astype(F32)
    bb_re = f_re[..., None] * br - f_im[..., None] * bi
    bb_im = f_re[..., None] * bi + f_im[..., None] * br
    cr, ci = c_re.astype(F32), c_im.astype(F32)

    cp_re = cr[None] * pw_re[:, :, :, None, :] - ci[None] * pw_im[:, :, :, None, :]
    cp_im = cr[None] * pw_im[:, :, :, None, :] + ci[None] * pw_re[:, :, :, None, :]
    taps = (jnp.einsum('kdgon,dgni->kdgoi', cp_re[:cs], bb_re, precision=hi)
            - jnp.einsum('kdgon,dgni->kdgoi', cp_im[:cs], bb_im, precision=hi))
    s_idx = jnp.arange(cs)[:, None]
    t_idx = jnp.arange(cs)[None, :]
    lag_f = jnp.clip(t_idx - s_idx, 0, cs - 1)
    lag_b = jnp.clip(s_idx - t_idx, 0, cs - 1)
    tf = jnp.where((t_idx >= s_idx)[None, :, :, None, None], taps[:, 0][lag_f].transpose(2, 0, 1, 3, 4), 0.0)
    tb = jnp.where((s_idx >= t_idx)[None, :, :, None, None], taps[:, 1][lag_b].transpose(2, 0, 1, 3, 4), 0.0)
    toep = (tf + tb).transpose(0, 1, 4, 2, 3).reshape(SSM_GROUPS, cs * p, cs * p)

    def state_cols(pw_r, pw_i, bbr, bbi):
        re = pw_r[..., None] * bbr[None] - pw_i[..., None] * bbi[None]
        im = pw_r[..., None] * bbi[None] + pw_i[..., None] * bbr[None]
        shape = (SSM_GROUPS, cs * p, n)
        return re.transpose(1, 0, 3, 2).reshape(shape), im.transpose(1, 0, 3, 2).reshape(shape)

    sf_re, sf_im = state_cols(pw_re[:cs, 0][::-1], pw_im[:cs, 0][::-1], bb_re[0], bb_im[0])
    sb_re, sb_im = state_cols(pw_re[:cs, 1], pw_im[:cs, 1], bb_re[1], bb_im[1])
    w_state = jnp.concatenate([sf_re, sb_re, sf_im, sb_im], axis=-1)

    def out_rows(cpr, cpi):
        shape = (SSM_GROUPS, n, cs * p)
        return cpr.transpose(1, 3, 0, 2).reshape(shape), (-cpi).transpose(1, 3, 0, 2).reshape(shape)

    of_re, of_im = out_rows(cp_re[1:, 0], cp_im[1:, 0])
    ob_re, ob_im = out_rows(cp_re[1:, 1][::-1], cp_im[1:, 1][::-1])
    w_out = jnp.concatenate([of_re, ob_re, of_im, ob_im], axis=1)

    a_re = jnp.concatenate([pw_re[cs, 0], pw_re[cs, 1]], axis=-1)
    a_im = jnp.concatenate([pw_im[cs, 0], pw_im[cs, 1]], axis=-1)
    a_pow = jnp.stack([a_re, a_im], axis=1)
    d_lanes = jnp.tile(d_skip.astype(F32).reshape(SSM_GROUPS, 1, p), (1, 1, cs))
    return toep.astype(BF16), w_state.astype(BF16), w_out.astype(BF16), a_pow, d_lanes


def _s5_kernel(u_ref, toep_ref, wst_ref, wout_ref, apow_ref, d_ref, z_ref,
               loc_ref, fwd_re_ref, fwd_im_ref, bwd_re_ref, bwd_im_ref):
    _, nch, tb, width = u_ref.shape
    u = u_ref[0].reshape(nch * tb, width)
    loc_ref[...] = _dot(u, wst_ref[0]).reshape(nch, tb, width)

    half = width // 2
    lane = lax.broadcasted_iota(jnp.int32, (1, half), 1)
    is_fwd = lane < SSM_STATE
    a_re = apow_ref[0, 0:1, :]
    a_im = apow_ref[0, 1:2, :]

    def step(i, carry):
        h_re, h_im = carry
        j = nch - 1 - i
        fwd_re_ref[i] = h_re
        fwd_im_ref[i] = h_im
        bwd_re_ref[j] = h_re
        bwd_im_ref[j] = h_im
        s_re = jnp.where(is_fwd, loc_ref[i, :, :half], loc_ref[j, :, :half])
        s_im = jnp.where(is_fwd, loc_ref[i, :, half:], loc_ref[j, :, half:])
        return (a_re * h_re - a_im * h_im + s_re, a_re * h_im + a_im * h_re + s_im)

    zero = jnp.zeros((tb, half), F32)
    lax.fori_loop(0, nch, step, (zero, zero))

    st_re = jnp.where(is_fwd, fwd_re_ref[...], bwd_re_ref[...]).reshape(nch * tb, half)
    st_im = jnp.where(is_fwd, fwd_im_ref[...], bwd_im_ref[...]).reshape(nch * tb, half)
    state = jnp.concatenate([st_re, st_im], axis=-1).astype(BF16)
    y = _dot(u, toep_ref[0]) + _dot(state, wout_ref[0]) + u.astype(F32) * d_ref[0]
    gelu = 0.5 * y * (1.0 + jnp.tanh(math.sqrt(2.0 / math.pi) * (y + 0.044715 * (y * y * y))))
    z_ref[0] = gelu.astype(BF16).reshape(nch, tb, width)


def _s5(u_g, toep, w_state, w_out, a_pow, d_lanes):
    groups, nch, b, width = u_g.shape
    tb = S5_BATCH_TILE if b % S5_BATCH_TILE == 0 else b
    io_spec = pl.BlockSpec((1, nch, tb, width), lambda g, i: (g, 0, i, 0))
    mat_spec = pl.BlockSpec((1, width, width), lambda g, i: (g, 0, 0))
    half = width // 2
    return pl.pallas_call(
        _s5_kernel,
        grid=(groups, b // tb),
        in_specs=[io_spec, mat_spec, mat_spec, mat_spec,
                  pl.BlockSpec((1, 2, half), lambda g, i: (g, 0, 0)),
                  pl.BlockSpec((1, 1, width), lambda g, i: (g, 0, 0))],
        out_specs=io_spec,
        out_shape=jax.ShapeDtypeStruct(u_g.shape, BF16),
        scratch_shapes=[pltpu.VMEM((nch, tb, width), F32)] + [pltpu.VMEM((nch, tb, half), F32)] * 4,
        compiler_params=_params("parallel", "parallel"),
        name="s5",
    )(u_g, toep, w_state, w_out, a_pow, d_lanes)


def _to_group_major(u, b, seq_len):
    nch = seq_len // S5_CHUNK
    u5 = u.reshape(b, nch, S5_CHUNK, SSM_GROUPS, SSM_GROUP)
    return u5.transpose(3, 1, 0, 2, 4).reshape(SSM_GROUPS, nch, b, S5_CHUNK * SSM_GROUP)


def _to_token_major(z_g, b, seq_len):
    nch = seq_len // S5_CHUNK
    z5 = z_g.reshape(SSM_GROUPS, nch, b, S5_CHUNK, SSM_GROUP)
    return z5.transpose(2, 1, 3, 0, 4).reshape(b * seq_len, SSM_WIDTH)


def _merge_ln_kernel(x_ref, oret_ref, z_ref, gr_ref, gs_ref, wo_ref, wglu_ref, wout_ref, g_ref, b_ref, o_ref):
    y_ret = _dot(oret_ref[...], wo_ref[...])
    val = _dot(z_ref[...], wglu_ref[:, :D_MODEL])
    gate = _dot(z_ref[...], wglu_ref[:, D_MODEL:])
    y_ssm = val * _sigmoid(gate)
    merged = gr_ref[...].astype(F32) * y_ret + gs_ref[...].astype(F32) * y_ssm
    mix = _dot(merged.astype(BF16), wout_ref[...])
    o_ref[...] = _layer_norm(ALPHA * x_ref[...] + mix, g_ref[...], b_ref[...])


def _merge_ln(x, oret, z, gr, gs, w_o, w_glu, w_out, ln_g, ln_b):
    t = x.shape[0]
    tile = pl.BlockSpec((TOKEN_TILE, D_MODEL), lambda i: (i, 0))
    return pl.pallas_call(
        _merge_ln_kernel,
        grid=(t // TOKEN_TILE,),
        in_specs=[tile] * 5 + [_resident(w_o.shape), _resident(w_glu.shape), _resident(w_out.shape),
                               _resident((1, D_MODEL)), _resident((1, D_MODEL))],
        out_specs=tile,
        out_shape=jax.ShapeDtypeStruct((t, D_MODEL), F32),
        compiler_params=_params("parallel"),
        name="merge_ln",
    )(x, oret, z, gr, gs, w_o, w_glu, w_out, ln_g.reshape(1, D_MODEL), ln_b.reshape(1, D_MODEL))


def _trunk(x, params):
    (ffn1_w_gu, ffn1_w_down, ln1_g, ln1_b, w_in, b_gate, ret_w_o, s5_lam_re, s5_lam_im, s5_log_dt,
     s5_b_re, s5_b_im, s5_c_re, s5_c_im, s5_d, s5_w_glu, w_out, ln2_g, ln2_b,
     ffn2_w_gu, ffn2_w_down, ln3_g, ln3_b) = params
    b, seq_len, _ = x.shape
    assert seq_len % TOKEN_TILE == 0 and seq_len % RET_CHUNK == 0 and seq_len % S5_CHUNK == 0
    cos_t, sin_t = _rotary_tables(seq_len)
    decay, edge = _retention_tables()
    x = x.reshape(b * seq_len, D_MODEL)
    bf = lambda w: w.astype(BF16)
    for l in range(DEPTH):
        x = _ffn_ln(x, bf(ffn1_w_gu[l]), bf(ffn1_w_down[l]), ln1_g[l], ln1_b[l])
        q, k, v, g, u, gr, gs = _in_proj(x, bf(w_in[l]), b_gate[l], cos_t, sin_t, seq_len)
        shape3 = lambda a: a.reshape(b, seq_len, a.shape[-1])
        oret = _retention(shape3(q), shape3(k), shape3(v), shape3(g), decay, edge)
        mats = _s5_matrices(s5_lam_re[l], s5_lam_im[l], s5_log_dt[l], s5_b_re[l], s5_b_im[l],
                            s5_c_re[l], s5_c_im[l], s5_d[l])
        z = _to_token_major(_s5(_to_group_major(u, b, seq_len), *mats), b, seq_len)
        x = _merge_ln(x, oret.reshape(b * seq_len, RET_V), z, gr, gs,
                      bf(ret_w_o[l]), bf(s5_w_glu[l]), bf(w_out[l]), ln2_g[l], ln2_b[l])
        x = _ffn_ln(x, bf(ffn2_w_gu[l]), bf(ffn2_w_down[l]), ln3_g[l], ln3_b[l])
    return x.reshape(b, seq_len, D_MODEL)


def kernel(x_prompt, x_sample, ffn1_w_gu, ffn1_w_down, ln1_g, ln1_b, w_in, b_gate, ret_w_o, s5_lam_re, s5_lam_im, s5_log_dt, s5_b_re, s5_b_im, s5_c_re, s5_c_im, s5_d, s5_w_glu, w_out, ln2_g, ln2_b, ffn2_w_gu, ffn2_w_down, ln3_g, ln3_b):
    params = (ffn1_w_gu, ffn1_w_down, ln1_g, ln1_b, w_in, b_gate, ret_w_o, s5_lam_re, s5_lam_im, s5_log_dt,
              s5_b_re, s5_b_im, s5_c_re, s5_c_im, s5_d, s5_w_glu, w_out, ln2_g, ln2_b,
              ffn2_w_gu, ffn2_w_down, ln3_g, ln3_b)
    assert x_prompt.shape[1:] == x_sample.shape[1:]
    n_prompt = x_prompt.shape[0]
    y = _trunk(jnp.concatenate([x_prompt, x_sample], axis=0), params)
    return (y[:n_prompt], y[n_prompt:])
```

```python
import functools
import math

import jax
import jax.numpy as jnp
from jax import lax
from jax.experimental import pallas as pl
from jax.experimental.pallas import tpu as pltpu

F32 = jnp.float32
BF16 = jnp.bfloat16

D_MODEL = 1024
DEPTH = 2
RET_HEADS = 8
RET_QK_DIM = 64
RET_V_DIM = 128
RET_QK = RET_HEADS * RET_QK_DIM
RET_V = RET_HEADS * RET_V_DIM
ROPE_BASE = 10000.0
SSM_WIDTH = 1024
SSM_GROUP = 16
SSM_GROUPS = SSM_WIDTH // SSM_GROUP
SSM_STATE = 64
EIG_CLIP = -1e-4
D_FF = 2816
ALPHA = (2 * DEPTH) ** 0.25
LN_EPS = 1e-5

LANES = 128
MXU_DIM = 256
VMEM_LIMIT_BYTES = 56 * 1024 * 1024

TOKEN_TILE = 512
FF_CHUNK = 704
RET_CHUNK = MXU_DIM
S5_CHUNK = MXU_DIM // SSM_GROUP
S5_BATCH_TILE = 16


def _resident(shape):
    nd = len(shape)
    return pl.BlockSpec(shape, lambda *_: (0,) * nd, pipeline_mode=pl.Buffered(1))


def _params(*sem):
    return pltpu.CompilerParams(dimension_semantics=sem, vmem_limit_bytes=VMEM_LIMIT_BYTES)


def _dot(a, b):
    return jnp.dot(a, b, preferred_element_type=F32)


def _sigmoid(x):
    return 1.0 / (1.0 + jnp.exp(-x))


def _layer_norm(y, g, b):
    mu = jnp.mean(y, axis=-1, keepdims=True)
    yc = y - mu
    var = jnp.mean(yc * yc, axis=-1, keepdims=True)
    return yc * lax.rsqrt(var + LN_EPS) * g + b


def _ffn_ln_kernel(x_ref, wgu_ref, wdn_ref, g_ref, b_ref, *rest):
    o_ref = rest[-1]
    x = x_ref[...]
    xb = x.astype(BF16)
    acc = jnp.zeros(x.shape, F32)
    for c in range(D_FF // FF_CHUNK):
        lo = c * FF_CHUNK
        a = _dot(xb, wgu_ref[:, lo:lo + FF_CHUNK])
        u = _dot(xb, wgu_ref[:, D_FF + lo:D_FF + lo + FF_CHUNK])
        h = (a * _sigmoid(a) * u).astype(BF16)
        acc = acc + _dot(h, wdn_ref[lo:lo + FF_CHUNK, :])
    o_ref[...] = _layer_norm(ALPHA * x + 0.5 * acc, g_ref[...], b_ref[...])


def _ffn_ln(x, w_gu, w_down, ln_g, ln_b, *, in_row0=0, rows=None, out_rows=None, out_row0=0, into=None):
    rows = x.shape[0] if rows is None else rows
    out_rows = rows if out_rows is None else out_rows
    in_t0, out_t0 = in_row0 // TOKEN_TILE, out_row0 // TOKEN_TILE
    args = [x, w_gu, w_down, ln_g.reshape(1, D_MODEL), ln_b.reshape(1, D_MODEL)]
    in_specs = [pl.BlockSpec((TOKEN_TILE, D_MODEL), lambda i: (i + in_t0, 0)),
                _resident(w_gu.shape), _resident(w_down.shape),
                _resident((1, D_MODEL)), _resident((1, D_MODEL))]
    aliases = {}
    if into is not None:
        args.append(into)
        in_specs.append(pl.BlockSpec(memory_space=pl.ANY))
        aliases = {len(args) - 1: 0}
        out_rows = into.shape[0]
    return pl.pallas_call(
        _ffn_ln_kernel,
        grid=(rows // TOKEN_TILE,),
        in_specs=in_specs,
        out_specs=pl.BlockSpec((TOKEN_TILE, D_MODEL), lambda i: (i + out_t0, 0)),
        out_shape=jax.ShapeDtypeStruct((out_rows, D_MODEL), F32),
        input_output_aliases=aliases,
        compiler_params=_params("parallel"),
        name="ffn_ln",
    )(*args)


def _in_proj_kernel(x_ref, w_ref, bg_ref, cos_ref, sin_ref,
                    q_ref, k_ref, v_ref, g_ref, u_ref, gr_ref, gs_ref):
    xb = x_ref[...].astype(BF16)

    def seg(lo, width):
        return _dot(xb, w_ref[:, lo:lo + width])

    lane = lax.broadcasted_iota(jnp.int32, (1, RET_QK), 1)
    first_half = (lane % RET_QK_DIM) < (RET_QK_DIM // 2)
    cos = cos_ref[...]
    sin = sin_ref[...]

    def rotary(t):
        half = RET_QK_DIM // 2
        swapped = jnp.where(first_half, pltpu.roll(t, RET_QK - half, axis=1), pltpu.roll(t, half, axis=1))
        return t * cos + swapped * sin

    q_ref[...] = rotary(seg(0, RET_QK)).astype(BF16)
    k_ref[...] = (rotary(seg(RET_QK, RET_QK)) * (RET_QK_DIM ** -0.5)).astype(BF16)
    base = 2 * RET_QK
    v_ref[...] = seg(base, RET_V).astype(BF16)
    g_ref[...] = seg(base + RET_V, RET_V).astype(BF16)
    u_ref[...] = seg(base + 2 * RET_V, SSM_WIDTH).astype(BF16)
    gate0 = base + 2 * RET_V + SSM_WIDTH
    gr_ref[...] = _sigmoid(seg(gate0, D_MODEL) + bg_ref[:, :D_MODEL]).astype(BF16)
    gs_ref[...] = _sigmoid(seg(gate0 + D_MODEL, D_MODEL) + bg_ref[:, D_MODEL:]).astype(BF16)


def _in_proj(x, w_in, b_gate, cos_t, sin_t, seq_len):
    t = x.shape[0]
    tiles_per_seq = seq_len // TOKEN_TILE
    tile = lambda w: pl.BlockSpec((TOKEN_TILE, w), lambda i: (i, 0))
    table = pl.BlockSpec((TOKEN_TILE, RET_QK), lambda i: (i % tiles_per_seq, 0))
    widths = (RET_QK, RET_QK, RET_V, RET_V, SSM_WIDTH, D_MODEL, D_MODEL)
    return pl.pallas_call(
        _in_proj_kernel,
        grid=(t // TOKEN_TILE,),
        in_specs=[tile(D_MODEL), _resident(w_in.shape), _resident((1, 2 * D_MODEL)), table, table],
        out_specs=[tile(w) for w in widths],
        out_shape=[jax.ShapeDtypeStruct((t, w), BF16) for w in widths],
        compiler_params=_params("parallel"),
        name="in_proj",
    )(x, w_in, b_gate.reshape(1, 2 * D_MODEL), cos_t, sin_t)


def _rotary_tables(seq_len):
    half = RET_QK_DIM // 2
    inv_freq = ROPE_BASE ** (-jnp.arange(half, dtype=F32) / half)
    ang = jnp.arange(seq_len, dtype=F32)[:, None] * inv_freq[None, :]
    cos = jnp.cos(ang)
    sin = jnp.sin(ang)
    cos_t = jnp.tile(jnp.concatenate([cos, cos], axis=-1), (1, RET_HEADS))
    sin_t = jnp.tile(jnp.concatenate([-sin, sin], axis=-1), (1, RET_HEADS))
    return cos_t, sin_t


def _retention_tables():
    c = RET_CHUNK
    log_gamma = jnp.log1p(-jnp.exp2(-5.0 - jnp.arange(RET_HEADS, dtype=F32)))
    idx = jnp.arange(c, dtype=F32)
    dist = jnp.abs(idx[:, None] - idx[None, :])
    decay = jnp.exp(log_gamma[:, None, None] * dist[None])
    expo = jnp.stack([idx + 1.0, c - idx, c - 1.0 - idx, idx], axis=0)
    edge = jnp.exp(log_gamma[:, None, None] * expo[None])
    edge = jnp.broadcast_to(edge[..., None], (RET_HEADS, 4, c, LANES))
    return decay, edge


def _retention_kernel(q_ref, k_ref, v_ref, g_ref, decay_ref, edge_ref, o_ref, kvf_ref, kvb_ref):
    c = RET_CHUNK
    nc = q_ref.shape[1] // c
    lane = lax.broadcasted_iota(jnp.int32, (1, LANES), 1)
    contract_rows = (((0,), (0,)), ((), ()))
    contract_lanes = (((1,), (1,)), ((), ()))
    for e in range(2):
        mine = (lane < RET_QK_DIM) if e == 0 else (lane >= RET_QK_DIM)
        vsl = slice(e * RET_V_DIM, (e + 1) * RET_V_DIM)
        xi_f, xi_b = edge_ref[e, 0], edge_ref[e, 1]
        zeta_f, zeta_b = edge_ref[e, 2], edge_ref[e, 3]
        chunk_decay = xi_f[c - 1:c, :]

        def k_masked(n):
            kc = k_ref[0, n * c:(n + 1) * c, :]
            return jnp.where(mine, kc, jnp.zeros_like(kc))

        state = jnp.zeros((LANES, RET_V_DIM), F32)
        for n in range(nc):
            kvf_ref[n] = state
            if n + 1 < nc:
                kz = (k_masked(n).astype(F32) * zeta_f).astype(BF16)
                kv = lax.dot_general(kz, v_ref[0, n * c:(n + 1) * c, vsl], contract_rows,
                                     preferred_element_type=F32)
                state = chunk_decay * state + kv
        state = jnp.zeros((LANES, RET_V_DIM), F32)
        for n in range(nc - 1, -1, -1):
            kvb_ref[n] = state
            if n > 0:
                kz = (k_masked(n).astype(F32) * zeta_b).astype(BF16)
                kv = lax.dot_general(kz, v_ref[0, n * c:(n + 1) * c, vsl], contract_rows,
                                     preferred_element_type=F32)
                state = chunk_decay * state + kv

        for n in range(nc):
            rows = slice(n * c, (n + 1) * c)
            qc = q_ref[0, rows, :]
            vc = v_ref[0, rows, vsl]
            scores = lax.dot_general(qc, k_masked(n), contract_lanes, preferred_element_type=F32)
            scores = scores * decay_ref[e]
            o = _dot(scores.astype(BF16), vc)
            qf = qc.astype(F32)
            o = o + _dot((qf * xi_f).astype(BF16), kvf_ref[n].astype(BF16))
            o = o + _dot((qf * xi_b).astype(BF16), kvb_ref[n].astype(BF16))
            mu = jnp.mean(o, axis=-1, keepdims=True)
            oc = o - mu
            var = jnp.mean(oc * oc, axis=-1, keepdims=True)
            on = oc * lax.rsqrt(var + LN_EPS)
            gate = g_ref[0, rows, vsl].astype(F32)
            o_ref[0, rows, vsl] = (gate * _sigmoid(gate) * on).astype(BF16)


def _retention(q, k, v, g, decay, edge):
    b, seq_len, _ = q.shape
    nc = seq_len // RET_CHUNK
    qk_spec = pl.BlockSpec((1, seq_len, LANES), lambda i, h: (i, 0, h))
    v_spec = pl.BlockSpec((1, seq_len, 2 * RET_V_DIM), lambda i, h: (i, 0, h))
    return pl.pallas_call(
        _retention_kernel,
        grid=(b, RET_HEADS // 2),
        in_specs=[qk_spec, qk_spec, v_spec, v_spec,
                  pl.BlockSpec((2, RET_CHUNK, RET_CHUNK), lambda i, h: (h, 0, 0)),
                  pl.BlockSpec((2, 4, RET_CHUNK, LANES), lambda i, h: (h, 0, 0, 0))],
        out_specs=v_spec,
        out_shape=jax.ShapeDtypeStruct((b, seq_len, RET_V), BF16),
        scratch_shapes=[pltpu.VMEM((nc, LANES, RET_V_DIM), F32), pltpu.VMEM((nc, LANES, RET_V_DIM), F32)],
        compiler_params=_params("parallel", "parallel"),
        name="retention",
    )(q, k, v, g, decay, edge)


def _s5_matrices(lam_re, lam_im, log_dt, b_re, b_im, c_re, c_im, d_skip):
    hi = lax.Precision.HIGHEST
    cs, n, p = S5_CHUNK, SSM_STATE, SSM_GROUP
    lr = jnp.minimum(lam_re.astype(F32), EIG_CLIP)
    li = lam_im.astype(F32)
    dt = jnp.exp(log_dt.astype(F32))[..., None]
    steps = jnp.arange(cs + 1, dtype=F32)[:, None, None, None]
    mag = jnp.exp(lr * dt * steps)
    pw_re = mag * jnp.cos(li * dt * steps)
    pw_im = mag * jnp.sin(li * dt * steps)
    ab_re, ab_im = pw_re[1], pw_im[1]
    den = lr * lr + li * li
    num_re = ab_re - 1.0
    f_re = (num_re * lr + ab_im * li) / den
    f_im = (ab_im * lr - num_re * li) / den
    br, bi = b_re.astype(F32), b_im.astype(F32)
    bb_re = f_re[..., None] * br - f_im[..., None] * bi
    bb_im = f_re[..., None] * bi + f_im[..., None] * br
    cr, ci = c_re.astype(F32), c_im.astype(F32)

    cp_re = cr[None] * pw_re[:, :, :, None, :] - ci[None] * pw_im[:, :, :, None, :]
    cp_im = cr[None] * pw_im[:, :, :, None, :] + ci[None] * pw_re[:, :, :, None, :]
    taps = (jnp.einsum('kdgon,dgni->kdgoi', cp_re[:cs], bb_re, precision=hi)
            - jnp.einsum('kdgon,dgni->kdgoi', cp_im[:cs], bb_im, precision=hi))
    s_idx = jnp.arange(cs)[:, None]
    t_idx = jnp.arange(cs)[None, :]
    lag_f = jnp.clip(t_idx - s_idx, 0, cs - 1)
    lag_b = jnp.clip(s_idx - t_idx, 0, cs - 1)
    tf = jnp.where((t_idx >= s_idx)[None, :, :, None, None], taps[:, 0][lag_f].transpose(2, 0, 1, 3, 4), 0.0)
    tb = jnp.where((s_idx >= t_idx)[None, :, :, None, None], taps[:, 1][lag_b].transpose(2, 0, 1, 3, 4), 0.0)
    toep = (tf + tb).transpose(0, 1, 4, 2, 3).reshape(SSM_GROUPS, cs * p, cs * p)

    def state_cols(pw_r, pw_i, bbr, bbi):
        re = pw_r[..., None] * bbr[None] - pw_i[..., None] * bbi[None]
        im = pw_r[..., None] * bbi[None] + pw_i[..., None] * bbr[None]
        shape = (SSM_GROUPS, cs * p, n)
        return re.transpose(1, 0, 3, 2).reshape(shape), im.transpose(1, 0, 3, 2).reshape(shape)

    sf_re, sf_im = state_cols(pw_re[:cs, 0][::-1], pw_im[:cs, 0][::-1], bb_re[0], bb_im[0])
    sb_re, sb_im = state_cols(pw_re[:cs, 1], pw_im[:cs, 1], bb_re[1], bb_im[1])
    w_state = jnp.concatenate([sf_re, sb_re, sf_im, sb_im], axis=-1)

    def out_rows(cpr, cpi):
        shape = (SSM_GROUPS, n, cs * p)
        return cpr.transpose(1, 3, 0, 2).reshape(shape), (-cpi).transpose(1, 3, 0, 2).reshape(shape)

    of_re, of_im = out_rows(cp_re[1:, 0], cp_im[1:, 0])
    ob_re, ob_im = out_rows(cp_re[1:, 1][::-1], cp_im[1:, 1][::-1])
    w_out = jnp.concatenate([of_re, ob_re, of_im, ob_im], axis=1)

    a_re = jnp.concatenate([pw_re[cs, 0], pw_re[cs, 1]], axis=-1)
    a_im = jnp.concatenate([pw_im[cs, 0], pw_im[cs, 1]], axis=-1)
    a_pow = jnp.stack([a_re, a_im], axis=1)
    d_lanes = jnp.tile(d_skip.astype(F32).reshape(SSM_GROUPS, 1, p), (1, 1, cs))
    return toep.astype(BF16), w_state.astype(BF16), w_out.astype(BF16), a_pow, d_lanes


def _s5_kernel(u_ref, toep_ref, wst_ref, wout_ref, apow_ref, d_ref, z_ref,
               loc_ref, fwd_re_ref, fwd_im_ref, bwd_re_ref, bwd_im_ref):
    _, nch, tb, width = u_ref.shape
    u = u_ref[0].reshape(nch * tb, width)
    loc_ref[...] = _dot(u, wst_ref[0]).reshape(nch, tb, width)

    half = width // 2
    lane = lax.broadcasted_iota(jnp.int32, (1, half), 1)
    is_fwd = lane < SSM_STATE
    a_re = apow_ref[0, 0:1, :]
    a_im = apow_ref[0, 1:2, :]

    def step(i, carry):
        h_re, h_im = carry
        j = nch - 1 - i
        fwd_re_ref[i] = h_re
        fwd_im_ref[i] = h_im
        bwd_re_ref[j] = h_re
        bwd_im_ref[j] = h_im
        s_re = jnp.where(is_fwd, loc_ref[i, :, :half], loc_ref[j, :, :half])
        s_im = jnp.where(is_fwd, loc_ref[i, :, half:], loc_ref[j, :, half:])
        return (a_re * h_re - a_im * h_im + s_re, a_re * h_im + a_im * h_re + s_im)

    zero = jnp.zeros((tb, half), F32)
    lax.fori_loop(0, nch, step, (zero, zero))

    st_re = jnp.where(is_fwd, fwd_re_ref[...], bwd_re_ref[...]).reshape(nch * tb, half)
    st_im = jnp.where(is_fwd, fwd_im_ref[...], bwd_im_ref[...]).reshape(nch * tb, half)
    state = jnp.concatenate([st_re, st_im], axis=-1).astype(BF16)
    y = _dot(u, toep_ref[0]) + _dot(state, wout_ref[0]) + u.astype(F32) * d_ref[0]
    gelu = 0.5 * y * (1.0 + jnp.tanh(math.sqrt(2.0 / math.pi) * (y + 0.044715 * (y * y * y))))
    z_ref[0] = gelu.astype(BF16).reshape(nch, tb, width)


def _s5(u_g, toep, w_state, w_out, a_pow, d_lanes):
    groups, nch, b, width = u_g.shape
    tb = S5_BATCH_TILE if b % S5_BATCH_TILE == 0 else b
    io_spec = pl.BlockSpec((1, nch, tb, width), lambda g, i: (g, 0, i, 0))
    mat_spec = pl.BlockSpec((1, width, width), lambda g, i: (g, 0, 0))
    half = width // 2
    return pl.pallas_call(
        _s5_kernel,
        grid=(groups, b // tb),
        in_specs=[io_spec, mat_spec, mat_spec, mat_spec,
                  pl.BlockSpec((1, 2, half), lambda g, i: (g, 0, 0)),
                  pl.BlockSpec((1, 1, width), lambda g, i: (g, 0, 0))],
        out_specs=io_spec,
        out_shape=jax.ShapeDtypeStruct(u_g.shape, BF16),
        scratch_shapes=[pltpu.VMEM((nch, tb, width), F32)] + [pltpu.VMEM((nch, tb, half), F32)] * 4,
        compiler_params=_params("parallel", "parallel"),
        name="s5",
    )(u_g, toep, w_state, w_out, a_pow, d_lanes)


GROUPS_PER_BLOCK = LANES // SSM_GROUP
REGROUP_WIDTH = S5_CHUNK * LANES
REGROUP_TOKENS = 512
STAGE_PAD = 8


def _selection_matrix():
    row = jnp.arange(REGROUP_WIDTH)
    step, rem = row // LANES, row % LANES
    col = (rem // SSM_GROUP) * (S5_CHUNK * SSM_GROUP) + step * SSM_GROUP + rem % SSM_GROUP
    return (col[:, None] == jnp.arange(REGROUP_WIDTH)[None, :]).astype(BF16)


def _to_groups_kernel(u_ref, sel_ref, o_ref, stage_ref, xcat_ref):
    tb, lc, _ = u_ref.shape
    nr = lc // S5_CHUNK
    slab = lc + STAGE_PAD
    for b in range(tb):
        stage_ref[b * slab:b * slab + lc, :] = u_ref[b].astype(F32)
    for r in range(nr):
        for s in range(S5_CHUNK):
            piece = stage_ref[pl.ds(r * S5_CHUNK + s, tb, stride=slab), :]
            xcat_ref[r * tb:(r + 1) * tb, s * LANES:(s + 1) * LANES] = piece.astype(BF16)
    grouped = _dot(xcat_ref[...], sel_ref[...]).astype(BF16)
    width = S5_CHUNK * SSM_GROUP
    for gam in range(GROUPS_PER_BLOCK):
        o_ref[gam] = grouped[:, gam * width:(gam + 1) * width].reshape(nr, tb, width)


def _to_tokens_kernel(z_ref, selt_ref, o_ref, stage_ref):
    _, nr, tb, width = z_ref.shape
    lc = nr * S5_CHUNK
    slab = lc + STAGE_PAD
    zcat = jnp.concatenate([z_ref[gam].reshape(nr * tb, width) for gam in range(GROUPS_PER_BLOCK)], axis=-1)
    zt = _dot(zcat, selt_ref[...])
    for r in range(nr):
        for t in range(S5_CHUNK):
            stage_ref[pl.ds(r * S5_CHUNK + t, tb, stride=slab), :] = zt[r * tb:(r + 1) * tb, t * LANES:(t + 1) * LANES]
    for b in range(tb):
        o_ref[b] = stage_ref[b * slab:b * slab + lc, :].astype(BF16)


def _regroup_specs(b, seq_len):
    tb = S5_BATCH_TILE if b % S5_BATCH_TILE == 0 else b
    lc = REGROUP_TOKENS
    grid = (b // tb, seq_len // lc, SSM_WIDTH // LANES)
    token_spec = pl.BlockSpec((tb, lc, LANES), lambda i, c, j: (i, c, j))
    group_spec = pl.BlockSpec((GROUPS_PER_BLOCK, lc // S5_CHUNK, tb, S5_CHUNK * SSM_GROUP),
                              lambda i, c, j: (j, c, i, 0))
    stage = pltpu.VMEM((tb * (lc + STAGE_PAD), LANES), F32)
    return tb, grid, token_spec, group_spec, stage


def _to_group_major(u, sel, b, seq_len):
    tb, grid, token_spec, group_spec, stage = _regroup_specs(b, seq_len)
    return pl.pallas_call(
        _to_groups_kernel,
        grid=grid,
        in_specs=[token_spec, _resident(sel.shape)],
        out_specs=group_spec,
        out_shape=jax.ShapeDtypeStruct((SSM_GROUPS, seq_len // S5_CHUNK, b, S5_CHUNK * SSM_GROUP), BF16),
        scratch_shapes=[stage, pltpu.VMEM((REGROUP_TOKENS // S5_CHUNK * tb, REGROUP_WIDTH), BF16)],
        compiler_params=_params("parallel", "parallel", "parallel"),
        name="to_groups",
    )(u, sel)


def _to_token_major(z_g, sel_t, b, seq_len):
    tb, grid, token_spec, group_spec, stage = _regroup_specs(b, seq_len)
    return pl.pallas_call(
        _to_tokens_kernel,
        grid=grid,
        in_specs=[group_spec, _resident(sel_t.shape)],
        out_specs=token_spec,
        out_shape=jax.ShapeDtypeStruct((b, seq_len, SSM_WIDTH), BF16),
        scratch_shapes=[stage],
        compiler_params=_params("parallel", "parallel", "parallel"),
        name="to_tokens",
    )(z_g, sel_t)


def _merge_ln_kernel(x_ref, oret_ref, z_ref, gr_ref, gs_ref, wo_ref, wglu_ref, wout_ref, g_ref, b_ref, o_ref):
    y_ret = _dot(oret_ref[...], wo_ref[...])
    val = _dot(z_ref[...], wglu_ref[:, :D_MODEL])
    gate = _dot(z_ref[...], wglu_ref[:, D_MODEL:])
    y_ssm = val * _sigmoid(gate)
    merged = gr_ref[...].astype(F32) * y_ret + gs_ref[...].astype(F32) * y_ssm
    mix = _dot(merged.astype(BF16), wout_ref[...])
    o_ref[...] = _layer_norm(ALPHA * x_ref[...] + mix, g_ref[...], b_ref[...])


def _merge_ln(x, oret, z, gr, gs, w_o, w_glu, w_out, ln_g, ln_b):
    t = x.shape[0]
    tile = pl.BlockSpec((TOKEN_TILE, D_MODEL), lambda i: (i, 0))
    return pl.pallas_call(
        _merge_ln_kernel,
        grid=(t // TOKEN_TILE,),
        in_specs=[tile] * 5 + [_resident(w_o.shape), _resident(w_glu.shape), _resident(w_out.shape),
                               _resident((1, D_MODEL)), _resident((1, D_MODEL))],
        out_specs=tile,
        out_shape=jax.ShapeDtypeStruct((t, D_MODEL), F32),
        compiler_params=_params("parallel"),
        name="merge_ln",
    )(x, oret, z, gr, gs, w_o, w_glu, w_out, ln_g.reshape(1, D_MODEL), ln_b.reshape(1, D_MODEL))


def _trunk(xs, params):
    (ffn1_w_gu, ffn1_w_down, ln1_g, ln1_b, w_in, b_gate, ret_w_o, s5_lam_re, s5_lam_im, s5_log_dt,
     s5_b_re, s5_b_im, s5_c_re, s5_c_im, s5_d, s5_w_glu, w_out, ln2_g, ln2_b,
     ffn2_w_gu, ffn2_w_down, ln3_g, ln3_b) = params
    seq_len = xs[0].shape[1]
    assert all(x.shape[1:] == (seq_len, D_MODEL) for x in xs)
    assert seq_len % TOKEN_TILE == 0 and seq_len % RET_CHUNK == 0 and seq_len % REGROUP_TOKENS == 0
    sizes = [x.shape[0] * seq_len for x in xs]
    starts = [sum(sizes[:i]) for i in range(len(xs))]
    b, t = sum(x.shape[0] for x in xs), sum(sizes)
    cos_t, sin_t = _rotary_tables(seq_len)
    decay, edge = _retention_tables()
    sel = _selection_matrix()
    bf = lambda w: w.astype(BF16)
    x = None
    for l in range(DEPTH):
        ffn1 = (bf(ffn1_w_gu[l]), bf(ffn1_w_down[l]), ln1_g[l], ln1_b[l])
        if l == 0:
            for xi, size, start in zip(xs, sizes, starts):
                x = _ffn_ln(xi.reshape(size, D_MODEL), *ffn1, out_rows=t, out_row0=start, into=x)
        else:
            x = _ffn_ln(x, *ffn1)
        q, k, v, g, u, gr, gs = _in_proj(x, bf(w_in[l]), b_gate[l], cos_t, sin_t, seq_len)
        shape3 = lambda a: a.reshape(b, seq_len, a.shape[-1])
        oret = _retention(shape3(q), shape3(k), shape3(v), shape3(g), decay, edge)
        mats = _s5_matrices(s5_lam_re[l], s5_lam_im[l], s5_log_dt[l], s5_b_re[l], s5_b_im[l],
                            s5_c_re[l], s5_c_im[l], s5_d[l])
        z = _to_token_major(_s5(_to_group_major(shape3(u), sel, b, seq_len), *mats), sel.T, b, seq_len)
        x = _merge_ln(x, oret.reshape(t, RET_V), z.reshape(t, SSM_WIDTH), gr, gs,
                      bf(ret_w_o[l]), bf(s5_w_glu[l]), bf(w_out[l]), ln2_g[l], ln2_b[l])
        ffn2 = (bf(ffn2_w_gu[l]), bf(ffn2_w_down[l]), ln3_g[l], ln3_b[l])
        if l < DEPTH - 1:
            x = _ffn_ln(x, *ffn2)
        else:
            ys = [_ffn_ln(x, *ffn2, in_row0=start, rows=size) for size, start in zip(sizes, starts)]
    return [y.reshape(xi.shape) for y, xi in zip(ys, xs)]


def kernel(x_prompt, x_sample, ffn1_w_gu, ffn1_w_down, ln1_g, ln1_b, w_in, b_gate, ret_w_o, s5_lam_re, s5_lam_im, s5_log_dt, s5_b_re, s5_b_im, s5_c_re, s5_c_im, s5_d, s5_w_glu, w_out, ln2_g, ln2_b, ffn2_w_gu, ffn2_w_down, ln3_g, ln3_b):
    params = (ffn1_w_gu, ffn1_w_down, ln1_g, ln1_b, w_in, b_gate, ret_w_o, s5_lam_re, s5_lam_im, s5_log_dt,
              s5_b_re, s5_b_im, s5_c_re, s5_c_im, s5_d, s5_w_glu, w_out, ln2_g, ln2_b,
              ffn2_w_gu, ffn2_w_down, ln3_g, ln3_b)
    y_prompt, y_sample = _trunk([x_prompt, x_sample], params)
    return (y_prompt, y_sample)
```

```python
import functools
import math

import jax
import jax.numpy as jnp
from jax import lax
from jax.experimental import pallas as pl
from jax.experimental.pallas import tpu as pltpu

F32 = jnp.float32
BF16 = jnp.bfloat16

D_MODEL = 1024
DEPTH = 2
RET_HEADS = 8
RET_QK_DIM = 64
RET_V_DIM = 128
RET_QK = RET_HEADS * RET_QK_DIM
RET_V = RET_HEADS * RET_V_DIM
ROPE_BASE = 10000.0
SSM_WIDTH = 1024
SSM_GROUP = 16
SSM_GROUPS = SSM_WIDTH // SSM_GROUP
SSM_STATE = 64
EIG_CLIP = -1e-4
D_FF = 2816
ALPHA = (2 * DEPTH) ** 0.25
LN_EPS = 1e-5

LANES = 128
MXU_DIM = 256
VMEM_LIMIT_BYTES = 56 * 1024 * 1024

TOKEN_TILE = 512
FF_CHUNKS = (1024, 1024, 768)
assert sum(FF_CHUNKS) == D_FF and all(w % MXU_DIM == 0 for w in FF_CHUNKS)
MERGE_COLS = 512
RET_CHUNK = MXU_DIM
S5_CHUNK = MXU_DIM // SSM_GROUP
S5_BATCH_TILE = 16


def _resident(shape):
    nd = len(shape)
    return pl.BlockSpec(shape, lambda *_: (0,) * nd, pipeline_mode=pl.Buffered(1))


def _params(*sem):
    return pltpu.CompilerParams(dimension_semantics=sem, vmem_limit_bytes=VMEM_LIMIT_BYTES)


def _dot(a, b):
    return jnp.dot(a, b, preferred_element_type=F32)


def _sigmoid(x):
    return 1.0 / (1.0 + jnp.exp(-x))


def _layer_norm(y, g, b):
    mu = jnp.mean(y, axis=-1, keepdims=True)
    yc = y - mu
    var = jnp.mean(yc * yc, axis=-1, keepdims=True)
    return yc * lax.rsqrt(var + LN_EPS) * g + b


def _ffn_ln_kernel(x_ref, wgu_ref, wdn_ref, g_ref, b_ref, *rest):
    o_ref = rest[-1]
    x = x_ref[...]
    xb = x.astype(BF16)
    acc = jnp.zeros(x.shape, F32)
    lo = 0
    for width in FF_CHUNKS:
        a = _dot(xb, wgu_ref[:, lo:lo + width])
        u = _dot(xb, wgu_ref[:, D_FF + lo:D_FF + lo + width])
        h = (a * _sigmoid(a) * u).astype(BF16)
        acc = acc + _dot(h, wdn_ref[lo:lo + width, :])
        lo += width
    o_ref[...] = _layer_norm(ALPHA * x + 0.5 * acc, g_ref[...], b_ref[...])


def _ffn_ln(x, w_gu, w_down, ln_g, ln_b, *, in_row0=0, rows=None, out_rows=None, out_row0=0, into=None):
    rows = x.shape[0] if rows is None else rows
    out_rows = rows if out_rows is None else out_rows
    in_t0, out_t0 = in_row0 // TOKEN_TILE, out_row0 // TOKEN_TILE
    args = [x, w_gu, w_down, ln_g.reshape(1, D_MODEL), ln_b.reshape(1, D_MODEL)]
    in_specs = [pl.BlockSpec((TOKEN_TILE, D_MODEL), lambda i: (i + in_t0, 0)),
                _resident(w_gu.shape), _resident(w_down.shape),
                _resident((1, D_MODEL)), _resident((1, D_MODEL))]
    aliases = {}
    if into is not None:
        args.append(into)
        in_specs.append(pl.BlockSpec(memory_space=pl.ANY))
        aliases = {len(args) - 1: 0}
        out_rows = into.shape[0]
    return pl.pallas_call(
        _ffn_ln_kernel,
        grid=(rows // TOKEN_TILE,),
        in_specs=in_specs,
        out_specs=pl.BlockSpec((TOKEN_TILE, D_MODEL), lambda i: (i + out_t0, 0)),
        out_shape=jax.ShapeDtypeStruct((out_rows, D_MODEL), F32),
        input_output_aliases=aliases,
        compiler_params=_params("parallel"),
        name="ffn_ln",
    )(*args)


def _in_proj_kernel(x_ref, w_ref, bg_ref, cos_ref, sin_ref,
                    q_ref, k_ref, v_ref, g_ref, u_ref, gr_ref, gs_ref):
    xb = x_ref[...].astype(BF16)

    def seg(lo, width):
        return _dot(xb, w_ref[:, lo:lo + width])

    lane = lax.broadcasted_iota(jnp.int32, (1, RET_QK), 1)
    first_half = (lane % RET_QK_DIM) < (RET_QK_DIM // 2)
    cos = cos_ref[...]
    sin = sin_ref[...]

    def rotary(t):
        half = RET_QK_DIM // 2
        swapped = jnp.where(first_half, pltpu.roll(t, RET_QK - half, axis=1), pltpu.roll(t, half, axis=1))
        return t * cos + swapped * sin

    q_ref[...] = rotary(seg(0, RET_QK)).astype(BF16)
    k_ref[...] = (rotary(seg(RET_QK, RET_QK)) * (RET_QK_DIM ** -0.5)).astype(BF16)
    base = 2 * RET_QK
    v_ref[...] = seg(base, RET_V).astype(BF16)
    g_ref[...] = seg(base + RET_V, RET_V).astype(BF16)
    u_ref[...] = seg(base + 2 * RET_V, SSM_WIDTH).astype(BF16)
    gate0 = base + 2 * RET_V + SSM_WIDTH
    gr_ref[...] = _sigmoid(seg(gate0, D_MODEL) + bg_ref[:, :D_MODEL]).astype(BF16)
    gs_ref[...] = _sigmoid(seg(gate0 + D_MODEL, D_MODEL) + bg_ref[:, D_MODEL:]).astype(BF16)


def _in_proj(x, w_in, b_gate, cos_t, sin_t, seq_len):
    t = x.shape[0]
    tiles_per_seq = seq_len // TOKEN_TILE
    tile = lambda w: pl.BlockSpec((TOKEN_TILE, w), lambda i: (i, 0))
    table = pl.BlockSpec((TOKEN_TILE, RET_QK), lambda i: (i % tiles_per_seq, 0))
    widths = (RET_QK, RET_QK, RET_V, RET_V, SSM_WIDTH, D_MODEL, D_MODEL)
    return pl.pallas_call(
        _in_proj_kernel,
        grid=(t // TOKEN_TILE,),
        in_specs=[tile(D_MODEL), _resident(w_in.shape), _resident((1, 2 * D_MODEL)), table, table],
        out_specs=[tile(w) for w in widths],
        out_shape=[jax.ShapeDtypeStruct((t, w), BF16) for w in widths],
        compiler_params=_params("parallel"),
        name="in_proj",
    )(x, w_in, b_gate.reshape(1, 2 * D_MODEL), cos_t, sin_t)


def _rotary_tables(seq_len):
    half = RET_QK_DIM // 2
    inv_freq = ROPE_BASE ** (-jnp.arange(half, dtype=F32) / half)
    ang = jnp.arange(seq_len, dtype=F32)[:, None] * inv_freq[None, :]
    cos = jnp.cos(ang)
    sin = jnp.sin(ang)
    cos_t = jnp.tile(jnp.concatenate([cos, cos], axis=-1), (1, RET_HEADS))
    sin_t = jnp.tile(jnp.concatenate([-sin, sin], axis=-1), (1, RET_HEADS))
    return cos_t, sin_t


def _retention_tables():
    c = RET_CHUNK
    log_gamma = jnp.log1p(-jnp.exp2(-5.0 - jnp.arange(RET_HEADS, dtype=F32)))
    idx = jnp.arange(c, dtype=F32)
    dist = jnp.abs(idx[:, None] - idx[None, :])
    decay = jnp.exp(log_gamma[:, None, None] * dist[None])
    expo = jnp.stack([idx + 1.0, c - idx, c - 1.0 - idx, idx], axis=0)
    edge = jnp.exp(log_gamma[:, None, None] * expo[None])
    edge = jnp.broadcast_to(edge[..., None], (RET_HEADS, 4, c, LANES))
    return decay, edge


def _retention_kernel(q_ref, k_ref, v_ref, g_ref, decay_ref, edge_ref, o_ref, kvf_ref, kvb_ref):
    c = RET_CHUNK
    nc = q_ref.shape[1] // c
    lane = lax.broadcasted_iota(jnp.int32, (1, LANES), 1)
    contract_rows = (((0,), (0,)), ((), ()))
    contract_lanes = (((1,), (1,)), ((), ()))
    for e in range(2):
        mine = (lane < RET_QK_DIM) if e == 0 else (lane >= RET_QK_DIM)
        vsl = slice(e * RET_V_DIM, (e + 1) * RET_V_DIM)
        xi_f, xi_b = edge_ref[e, 0], edge_ref[e, 1]
        zeta_f, zeta_b = edge_ref[e, 2], edge_ref[e, 3]
        chunk_decay = xi_f[c - 1:c, :]

        def k_masked(n):
            kc = k_ref[0, n * c:(n + 1) * c, :]
            return jnp.where(mine, kc, jnp.zeros_like(kc))

        state = jnp.zeros((LANES, RET_V_DIM), F32)
        for n in range(nc):
            kvf_ref[n] = state
            if n + 1 < nc:
                kz = (k_masked(n).astype(F32) * zeta_f).astype(BF16)
                kv = lax.dot_general(kz, v_ref[0, n * c:(n + 1) * c, vsl], contract_rows,
                                     preferred_element_type=F32)
                state = chunk_decay * state + kv
        state = jnp.zeros((LANES, RET_V_DIM), F32)
        for n in range(nc - 1, -1, -1):
            kvb_ref[n] = state
            if n > 0:
                kz = (k_masked(n).astype(F32) * zeta_b).astype(BF16)
                kv = lax.dot_general(kz, v_ref[0, n * c:(n + 1) * c, vsl], contract_rows,
                                     preferred_element_type=F32)
                state = chunk_decay * state + kv

        for n in range(nc):
            rows = slice(n * c, (n + 1) * c)
            qc = q_ref[0, rows, :]
            vc = v_ref[0, rows, vsl]
            scores = lax.dot_general(qc, k_masked(n), contract_lanes, preferred_element_type=F32)
            scores = scores * decay_ref[e]
            o = _dot(scores.astype(BF16), vc)
            qf = qc.astype(F32)
            o = o + _dot((qf * xi_f).astype(BF16), kvf_ref[n].astype(BF16))
            o = o + _dot((qf * xi_b).astype(BF16), kvb_ref[n].astype(BF16))
            mu = jnp.mean(o, axis=-1, keepdims=True)
            oc = o - mu
            var = jnp.mean(oc * oc, axis=-1, keepdims=True)
            on = oc * lax.rsqrt(var + LN_EPS)
            gate = g_ref[0, rows, vsl].astype(F32)
            o_ref[0, rows, vsl] = (gate * _sigmoid(gate) * on).astype(BF16)


def _retention(q, k, v, g, decay, edge):
    b, seq_len, _ = q.shape
    nc = seq_len // RET_CHUNK
    qk_spec = pl.BlockSpec((1, seq_len, LANES), lambda i, h: (i, 0, h))
    v_spec = pl.BlockSpec((1, seq_len, 2 * RET_V_DIM), lambda i, h: (i, 0, h))
    return pl.pallas_call(
        _retention_kernel,
        grid=(b, RET_HEADS // 2),
        in_specs=[qk_spec, qk_spec, v_spec, v_spec,
                  pl.BlockSpec((2, RET_CHUNK, RET_CHUNK), lambda i, h: (h, 0, 0)),
                  pl.BlockSpec((2, 4, RET_CHUNK, LANES), lambda i, h: (h, 0, 0, 0))],
        out_specs=v_spec,
        out_shape=jax.ShapeDtypeStruct((b, seq_len, RET_V), BF16),
        scratch_shapes=[pltpu.VMEM((nc, LANES, RET_V_DIM), F32), pltpu.VMEM((nc, LANES, RET_V_DIM), F32)],
        compiler_params=_params("parallel", "parallel"),
        name="retention",
    )(q, k, v, g, decay, edge)


def _s5_matrices(lam_re, lam_im, log_dt, b_re, b_im, c_re, c_im, d_skip):
    hi = lax.Precision.HIGHEST
    cs, n, p = S5_CHUNK, SSM_STATE, SSM_GROUP
    lr = jnp.minimum(lam_re.astype(F32), EIG_CLIP)
    li = lam_im.astype(F32)
    dt = jnp.exp(log_dt.astype(F32))[..., None]
    groups, width = SSM_GROUPS, cs * p
    steps = jnp.arange(cs + 1, dtype=F32)[None, None, :, None]
    mag = jnp.exp((lr * dt)[:, :, None, :] * steps)
    ang = (li * dt)[:, :, None, :] * steps
    pw_re = mag * jnp.cos(ang)
    pw_im = mag * jnp.sin(ang)
    ab_re, ab_im = pw_re[:, :, 1], pw_im[:, :, 1]
    den = lr * lr + li * li
    num_re = ab_re - 1.0
    f_re = ((num_re * lr + ab_im * li) / den)[:, :, None, :]
    f_im = ((ab_im * lr - num_re * li) / den)[:, :, None, :]
    br_t = b_re.astype(F32).transpose(0, 1, 3, 2)
    bi_t = b_im.astype(F32).transpose(0, 1, 3, 2)
    bb_re = f_re * br_t - f_im * bi_t
    bb_im = f_re * bi_t + f_im * br_t
    cr_t = c_re.astype(F32).transpose(0, 1, 3, 2)[:, :, :, None, :]
    ci_t = c_im.astype(F32).transpose(0, 1, 3, 2)[:, :, :, None, :]
    pwn_re = pw_re.transpose(0, 1, 3, 2)[..., None]
    pwn_im = pw_im.transpose(0, 1, 3, 2)[..., None]
    cp_re = cr_t * pwn_re - ci_t * pwn_im
    cp_im = cr_t * pwn_im + ci_t * pwn_re

    lhs = jnp.concatenate([bb_re, -bb_im], axis=-1)
    rhs = jnp.concatenate([cp_re[:, :, :, :cs], cp_im[:, :, :, :cs]], axis=2).reshape(2, groups, 2 * n, width)
    taps = jnp.einsum('dgin,dgnl->dgil', lhs, rhs, precision=hi)
    zeros = jnp.zeros((groups, p, width), F32)
    pad_f = jnp.concatenate([zeros, taps[0]], axis=-1)
    pad_b = jnp.concatenate([taps[1].reshape(groups, p, cs, p)[:, :, ::-1].reshape(groups, p, width), zeros], axis=-1)
    toep = jnp.stack([pad_f[:, :, width - p * s:2 * width - p * s]
                      + pad_b[:, :, p * (cs - 1 - s):p * (cs - 1 - s) + width] for s in range(cs)], axis=1)
    toep = toep.reshape(groups, width, width)

    def state_cols(pw_r, pw_i, bbr, bbi):
        pw_r, pw_i, bbr, bbi = pw_r[:, :, None, :], pw_i[:, :, None, :], bbr[:, None], bbi[:, None]
        shape = (groups, width, n)
        return (pw_r * bbr - pw_i * bbi).reshape(shape), (pw_r * bbi + pw_i * bbr).reshape(shape)

    sf_re, sf_im = state_cols(pw_re[0, :, :cs][:, ::-1], pw_im[0, :, :cs][:, ::-1], bb_re[0], bb_im[0])
    sb_re, sb_im = state_cols(pw_re[1, :, :cs], pw_im[1, :, :cs], bb_re[1], bb_im[1])
    w_state = jnp.concatenate([sf_re, sb_re, sf_im, sb_im], axis=-1)

    rows = lambda a: a.reshape(groups, n, width)
    w_out = jnp.concatenate([rows(cp_re[0, :, :, 1:]), rows(cp_re[1, :, :, 1:][:, :, ::-1]),
                             rows(-cp_im[0, :, :, 1:]), rows(-cp_im[1, :, :, 1:][:, :, ::-1])], axis=1)

    a_re = jnp.concatenate([pw_re[0, :, cs], pw_re[1, :, cs]], axis=-1)
    a_im = jnp.concatenate([pw_im[0, :, cs], pw_im[1, :, cs]], axis=-1)
    a_pow = jnp.stack([a_re, a_im], axis=1)
    d_lanes = jnp.tile(d_skip.astype(F32).reshape(SSM_GROUPS, 1, p), (1, 1, cs))
    return toep.astype(BF16), w_state.astype(BF16), w_out.astype(BF16), a_pow, d_lanes


def _s5_kernel(u_ref, toep_ref, wst_ref, wout_ref, apow_ref, d_ref, z_ref,
               loc_ref, fwd_re_ref, fwd_im_ref, bwd_re_ref, bwd_im_ref):
    _, nch, tb, width = u_ref.shape
    u = u_ref[0].reshape(nch * tb, width)
    loc_ref[...] = _dot(u, wst_ref[0]).reshape(nch, tb, width)

    half = width // 2
    lane = lax.broadcasted_iota(jnp.int32, (1, half), 1)
    is_fwd = lane < SSM_STATE
    a_re = apow_ref[0, 0:1, :]
    a_im = apow_ref[0, 1:2, :]

    def step(i, carry):
        h_re, h_im = carry
        j = nch - 1 - i
        fwd_re_ref[i] = h_re
        fwd_im_ref[i] = h_im
        bwd_re_ref[j] = h_re
        bwd_im_ref[j] = h_im
        s_re = jnp.where(is_fwd, loc_ref[i, :, :half], loc_ref[j, :, :half])
        s_im = jnp.where(is_fwd, loc_ref[i, :, half:], loc_ref[j, :, half:])
        return (a_re * h_re - a_im * h_im + s_re, a_re * h_im + a_im * h_re + s_im)

    zero = jnp.zeros((tb, half), F32)
    lax.fori_loop(0, nch, step, (zero, zero))

    st_re = jnp.where(is_fwd, fwd_re_ref[...], bwd_re_ref[...]).reshape(nch * tb, half)
    st_im = jnp.where(is_fwd, fwd_im_ref[...], bwd_im_ref[...]).reshape(nch * tb, half)
    state = jnp.concatenate([st_re, st_im], axis=-1).astype(BF16)
    y = _dot(u, toep_ref[0]) + _dot(state, wout_ref[0]) + u.astype(F32) * d_ref[0]
    gelu = 0.5 * y * (1.0 + jnp.tanh(math.sqrt(2.0 / math.pi) * (y + 0.044715 * (y * y * y))))
    z_ref[0] = gelu.astype(BF16).reshape(nch, tb, width)


def _s5(u_g, toep, w_state, w_out, a_pow, d_lanes):
    groups, nch, b, width = u_g.shape
    tb = S5_BATCH_TILE if b % S5_BATCH_TILE == 0 else b
    io_spec = pl.BlockSpec((1, nch, tb, width), lambda g, i: (g, 0, i, 0))
    mat_spec = pl.BlockSpec((1, width, width), lambda g, i: (g, 0, 0))
    half = width // 2
    return pl.pallas_call(
        _s5_kernel,
        grid=(groups, b // tb),
        in_specs=[io_spec, mat_spec, mat_spec, mat_spec,
                  pl.BlockSpec((1, 2, half), lambda g, i: (g, 0, 0)),
                  pl.BlockSpec((1, 1, width), lambda g, i: (g, 0, 0))],
        out_specs=io_spec,
        out_shape=jax.ShapeDtypeStruct(u_g.shape, BF16),
        scratch_shapes=[pltpu.VMEM((nch, tb, width), F32)] + [pltpu.VMEM((nch, tb, half), F32)] * 4,
        compiler_params=_params("parallel", "parallel"),
        name="s5",
    )(u_g, toep, w_state, w_out, a_pow, d_lanes)


GROUPS_PER_BLOCK = LANES // SSM_GROUP
REGROUP_WIDTH = S5_CHUNK * LANES
REGROUP_TOKENS = 512
STAGE_PAD = 8


def _selection_matrix():
    row = jnp.arange(REGROUP_WIDTH)
    step, rem = row // LANES, row % LANES
    col = (rem // SSM_GROUP) * (S5_CHUNK * SSM_GROUP) + step * SSM_GROUP + rem % SSM_GROUP
    return (col[:, None] == jnp.arange(REGROUP_WIDTH)[None, :]).astype(BF16)


def _to_groups_kernel(u_ref, sel_ref, o_ref, stage_ref, xcat_ref):
    tb, lc, _ = u_ref.shape
    nr = lc // S5_CHUNK
    slab = lc + STAGE_PAD
    for b in range(tb):
        stage_ref[b * slab:b * slab + lc, :] = u_ref[b].astype(F32)
    for r in range(nr):
        for s in range(S5_CHUNK):
            piece = stage_ref[pl.ds(r * S5_CHUNK + s, tb, stride=slab), :]
            xcat_ref[r * tb:(r + 1) * tb, s * LANES:(s + 1) * LANES] = piece.astype(BF16)
    grouped = _dot(xcat_ref[...], sel_ref[...]).astype(BF16)
    width = S5_CHUNK * SSM_GROUP
    for gam in range(GROUPS_PER_BLOCK):
        o_ref[gam] = grouped[:, gam * width:(gam + 1) * width].reshape(nr, tb, width)


def _to_tokens_kernel(z_ref, selt_ref, o_ref, stage_ref):
    _, nr, tb, width = z_ref.shape
    lc = nr * S5_CHUNK
    slab = lc + STAGE_PAD
    zcat = jnp.concatenate([z_ref[gam].reshape(nr * tb, width) for gam in range(GROUPS_PER_BLOCK)], axis=-1)
    zt = _dot(zcat, selt_ref[...])
    for r in range(nr):
        for t in range(S5_CHUNK):
            stage_ref[pl.ds(r * S5_CHUNK + t, tb, stride=slab), :] = zt[r * tb:(r + 1) * tb, t * LANES:(t + 1) * LANES]
    for b in range(tb):
        o_ref[b] = stage_ref[b * slab:b * slab + lc, :].astype(BF16)


def _regroup_specs(b, seq_len):
    tb = S5_BATCH_TILE if b % S5_BATCH_TILE == 0 else b
    lc = REGROUP_TOKENS
    grid = (b // tb, seq_len // lc, SSM_WIDTH // LANES)
    token_spec = pl.BlockSpec((tb, lc, LANES), lambda i, c, j: (i, c, j))
    group_spec = pl.BlockSpec((GROUPS_PER_BLOCK, lc // S5_CHUNK, tb, S5_CHUNK * SSM_GROUP),
                              lambda i, c, j: (j, c, i, 0))
    stage = pltpu.VMEM((tb * (lc + STAGE_PAD), LANES), F32)
    return tb, grid, token_spec, group_spec, stage


def _to_group_major(u, sel, b, seq_len):
    tb, grid, token_spec, group_spec, stage = _regroup_specs(b, seq_len)
    return pl.pallas_call(
        _to_groups_kernel,
        grid=grid,
        in_specs=[token_spec, _resident(sel.shape)],
        out_specs=group_spec,
        out_shape=jax.ShapeDtypeStruct((SSM_GROUPS, seq_len // S5_CHUNK, b, S5_CHUNK * SSM_GROUP), BF16),
        scratch_shapes=[stage, pltpu.VMEM((REGROUP_TOKENS // S5_CHUNK * tb, REGROUP_WIDTH), BF16)],
        compiler_params=_params("parallel", "parallel", "parallel"),
        name="to_groups",
    )(u, sel)


def _to_token_major(z_g, sel_t, b, seq_len):
    tb, grid, token_spec, group_spec, stage = _regroup_specs(b, seq_len)
    return pl.pallas_call(
        _to_tokens_kernel,
        grid=grid,
        in_specs=[group_spec, _resident(sel_t.shape)],
        out_specs=token_spec,
        out_shape=jax.ShapeDtypeStruct((b, seq_len, SSM_WIDTH), BF16),
        scratch_shapes=[stage],
        compiler_params=_params("parallel", "parallel", "parallel"),
        name="to_tokens",
    )(z_g, sel_t)


def _merge_ln_kernel(x_ref, oret_ref, z_ref, gr_ref, gs_ref, wo_ref, wglu_ref, wout_ref, g_ref, b_ref, o_ref):
    oret = oret_ref[...]
    z = z_ref[...]
    pieces = []
    for lo in range(0, D_MODEL, MERGE_COLS):
        cols = slice(lo, lo + MERGE_COLS)
        y_ret = _dot(oret, wo_ref[:, cols])
        val = _dot(z, wglu_ref[:, cols])
        gate = _dot(z, wglu_ref[:, D_MODEL + lo:D_MODEL + lo + MERGE_COLS])
        y_ssm = val * _sigmoid(gate)
        pieces.append((gr_ref[:, cols].astype(F32) * y_ret + gs_ref[:, cols].astype(F32) * y_ssm).astype(BF16))
    mix = _dot(jnp.concatenate(pieces, axis=-1), wout_ref[...])
    o_ref[...] = _layer_norm(ALPHA * x_ref[...] + mix, g_ref[...], b_ref[...])


def _merge_ln(x, oret, z, gr, gs, w_o, w_glu, w_out, ln_g, ln_b):
    t = x.shape[0]
    tile = pl.BlockSpec((TOKEN_TILE, D_MODEL), lambda i: (i, 0))
    return pl.pallas_call(
        _merge_ln_kernel,
        grid=(t // TOKEN_TILE,),
        in_specs=[tile] * 5 + [_resident(w_o.shape), _resident(w_glu.shape), _resident(w_out.shape),
                               _resident((1, D_MODEL)), _resident((1, D_MODEL))],
        out_specs=tile,
        out_shape=jax.ShapeDtypeStruct((t, D_MODEL), F32),
        compiler_params=_params("parallel"),
        name="merge_ln",
    )(x, oret, z, gr, gs, w_o, w_glu, w_out, ln_g.reshape(1, D_MODEL), ln_b.reshape(1, D_MODEL))


def _trunk(xs, params):
    (ffn1_w_gu, ffn1_w_down, ln1_g, ln1_b, w_in, b_gate, ret_w_o, s5_lam_re, s5_lam_im, s5_log_dt,
     s5_b_re, s5_b_im, s5_c_re, s5_c_im, s5_d, s5_w_glu, w_out, ln2_g, ln2_b,
     ffn2_w_gu, ffn2_w_down, ln3_g, ln3_b) = params
    seq_len = xs[0].shape[1]
    assert all(x.shape[1:] == (seq_len, D_MODEL) for x in xs)
    assert seq_len % TOKEN_TILE == 0 and seq_len % RET_CHUNK == 0 and seq_len % REGROUP_TOKENS == 0
    sizes = [x.shape[0] * seq_len for x in xs]
    starts = [sum(sizes[:i]) for i in range(len(xs))]
    b, t = sum(x.shape[0] for x in xs), sum(sizes)
    cos_t, sin_t = _rotary_tables(seq_len)
    decay, edge = _retention_tables()
    sel = _selection_matrix()
    bf = lambda w: w.astype(BF16)
    x = None
    for l in range(DEPTH):
        ffn1 = (bf(ffn1_w_gu[l]), bf(ffn1_w_down[l]), ln1_g[l], ln1_b[l])
        if l == 0:
            for xi, size, start in zip(xs, sizes, starts):
                x = _ffn_ln(xi.reshape(size, D_MODEL), *ffn1, out_rows=t, out_row0=start, into=x)
        else:
            x = _ffn_ln(x, *ffn1)
        q, k, v, g, u, gr, gs = _in_proj(x, bf(w_in[l]), b_gate[l], cos_t, sin_t, seq_len)
        shape3 = lambda a: a.reshape(b, seq_len, a.shape[-1])
        oret = _retention(shape3(q), shape3(k), shape3(v), shape3(g), decay, edge)
        mats = _s5_matrices(s5_lam_re[l], s5_lam_im[l], s5_log_dt[l], s5_b_re[l], s5_b_im[l],
                            s5_c_re[l], s5_c_im[l], s5_d[l])
        z = _to_token_major(_s5(_to_group_major(shape3(u), sel, b, seq_len), *mats), sel.T, b, seq_len)
        x = _merge_ln(x, oret.reshape(t, RET_V), z.reshape(t, SSM_WIDTH), gr, gs,
                      bf(ret_w_o[l]), bf(s5_w_glu[l]), bf(w_out[l]), ln2_g[l], ln2_b[l])
        ffn2 = (bf(ffn2_w_gu[l]), bf(ffn2_w_down[l]), ln3_g[l], ln3_b[l])
        if l < DEPTH - 1:
            x = _ffn_ln(x, *ffn2)
        else:
            ys = [_ffn_ln(x, *ffn2, in_row0=start, rows=size) for size, start in zip(sizes, starts)]
    return [y.reshape(xi.shape) for y, xi in zip(ys, xs)]


def kernel(x_prompt, x_sample, ffn1_w_gu, ffn1_w_down, ln1_g, ln1_b, w_in, b_gate, ret_w_o, s5_lam_re, s5_lam_im, s5_log_dt, s5_b_re, s5_b_im, s5_c_re, s5_c_im, s5_d, s5_w_glu, w_out, ln2_g, ln2_b, ffn2_w_gu, ffn2_w_down, ln3_g, ln3_b):
    params = (ffn1_w_gu, ffn1_w_down, ln1_g, ln1_b, w_in, b_gate, ret_w_o, s5_lam_re, s5_lam_im, s5_log_dt,
              s5_b_re, s5_b_im, s5_c_re, s5_c_im, s5_d, s5_w_glu, w_out, ln2_g, ln2_b,
              ffn2_w_gu, ffn2_w_down, ln3_g, ln3_b)
    y_prompt, y_sample = _trunk([x_prompt, x_sample], params)
    return (y_prompt, y_sample)
```

```python
import functools
import math

import jax
import jax.numpy as jnp
from jax import lax
from jax.experimental import pallas as pl
from jax.experimental.pallas import tpu as pltpu

F32 = jnp.float32
BF16 = jnp.bfloat16

D_MODEL = 1024
DEPTH = 2
RET_HEADS = 8
RET_QK_DIM = 64
RET_V_DIM = 128
RET_QK = RET_HEADS * RET_QK_DIM
RET_V = RET_HEADS * RET_V_DIM
ROPE_BASE = 10000.0
SSM_WIDTH = 1024
SSM_GROUP = 16
SSM_GROUPS = SSM_WIDTH // SSM_GROUP
SSM_STATE = 64
EIG_CLIP = -1e-4
D_FF = 2816
ALPHA = (2 * DEPTH) ** 0.25
LN_EPS = 1e-5

LANES = 128
MXU_DIM = 256
VMEM_LIMIT_BYTES = 56 * 1024 * 1024

TOKEN_TILE = 512
FF_CHUNKS = (1024, 1024, 768)
assert sum(FF_CHUNKS) == D_FF and all(w % MXU_DIM == 0 for w in FF_CHUNKS)
MERGE_COLS = 512
RET_CHUNK = MXU_DIM
S5_CHUNK = MXU_DIM // SSM_GROUP
S5_BATCH_TILE = 16


def _resident(shape):
    nd = len(shape)
    return pl.BlockSpec(shape, lambda *_: (0,) * nd, pipeline_mode=pl.Buffered(1))


def _params(*sem):
    return pltpu.CompilerParams(dimension_semantics=sem, vmem_limit_bytes=VMEM_LIMIT_BYTES)


def _dot(a, b):
    return jnp.dot(a, b, preferred_element_type=F32)


def _sigmoid(x):
    return 1.0 / (1.0 + jnp.exp(-x))


def _layer_norm(y, g, b):
    mu = jnp.mean(y, axis=-1, keepdims=True)
    yc = y - mu
    var = jnp.mean(yc * yc, axis=-1, keepdims=True)
    return yc * lax.rsqrt(var + LN_EPS) * g + b


def _ffn_ln_kernel(x_ref, wgu_ref, wdn_ref, g_ref, b_ref, *rest):
    o_ref = rest[-1]
    x = x_ref[...]
    xb = x.astype(BF16)
    acc = jnp.zeros(x.shape, F32)
    lo = 0
    for width in FF_CHUNKS:
        a = _dot(xb, wgu_ref[:, lo:lo + width])
        u = _dot(xb, wgu_ref[:, D_FF + lo:D_FF + lo + width])
        h = (a * _sigmoid(a) * u).astype(BF16)
        acc = acc + _dot(h, wdn_ref[lo:lo + width, :])
        lo += width
    o_ref[...] = _layer_norm(ALPHA * x + 0.5 * acc, g_ref[...], b_ref[...])


def _ffn_ln(x, w_gu, w_down, ln_g, ln_b, *, in_row0=0, rows=None, out_rows=None, out_row0=0, into=None):
    rows = x.shape[0] if rows is None else rows
    out_rows = rows if out_rows is None else out_rows
    in_t0, out_t0 = in_row0 // TOKEN_TILE, out_row0 // TOKEN_TILE
    args = [x, w_gu, w_down, ln_g.reshape(1, D_MODEL), ln_b.reshape(1, D_MODEL)]
    in_specs = [pl.BlockSpec((TOKEN_TILE, D_MODEL), lambda i: (i + in_t0, 0)),
                _resident(w_gu.shape), _resident(w_down.shape),
                _resident((1, D_MODEL)), _resident((1, D_MODEL))]
    aliases = {}
    if into is not None:
        args.append(into)
        in_specs.append(pl.BlockSpec(memory_space=pl.ANY))
        aliases = {len(args) - 1: 0}
        out_rows = into.shape[0]
    return pl.pallas_call(
        _ffn_ln_kernel,
        grid=(rows // TOKEN_TILE,),
        in_specs=in_specs,
        out_specs=pl.BlockSpec((TOKEN_TILE, D_MODEL), lambda i: (i + out_t0, 0)),
        out_shape=jax.ShapeDtypeStruct((out_rows, D_MODEL), F32),
        input_output_aliases=aliases,
        compiler_params=_params("parallel"),
        name="ffn_ln",
    )(*args)


def _in_proj_kernel(x_ref, w_ref, bg_ref, cos_ref, sin_ref,
                    q_ref, k_ref, v_ref, g_ref, u_ref, gr_ref, gs_ref):
    xb = x_ref[...].astype(BF16)

    def seg(lo, width):
        return _dot(xb, w_ref[:, lo:lo + width])

    lane = lax.broadcasted_iota(jnp.int32, (1, RET_QK), 1)
    first_half = (lane % RET_QK_DIM) < (RET_QK_DIM // 2)
    cos = cos_ref[...]
    sin = sin_ref[...]

    def rotary(t):
        half = RET_QK_DIM // 2
        swapped = jnp.where(first_half, pltpu.roll(t, RET_QK - half, axis=1), pltpu.roll(t, half, axis=1))
        return t * cos + swapped * sin

    q_ref[...] = rotary(seg(0, RET_QK)).astype(BF16)
    k_ref[...] = (rotary(seg(RET_QK, RET_QK)) * (RET_QK_DIM ** -0.5)).astype(BF16)
    base = 2 * RET_QK
    v_ref[...] = seg(base, RET_V).astype(BF16)
    g_ref[...] = seg(base + RET_V, RET_V).astype(BF16)
    u_ref[...] = seg(base + 2 * RET_V, SSM_WIDTH).astype(BF16)
    gate0 = base + 2 * RET_V + SSM_WIDTH
    gr_ref[...] = _sigmoid(seg(gate0, D_MODEL) + bg_ref[:, :D_MODEL]).astype(BF16)
    gs_ref[...] = _sigmoid(seg(gate0 + D_MODEL, D_MODEL) + bg_ref[:, D_MODEL:]).astype(BF16)


def _in_proj(x, w_in, b_gate, cos_t, sin_t, seq_len):
    t = x.shape[0]
    tiles_per_seq = seq_len // TOKEN_TILE
    tile = lambda w: pl.BlockSpec((TOKEN_TILE, w), lambda i: (i, 0))
    table = pl.BlockSpec((TOKEN_TILE, RET_QK), lambda i: (i % tiles_per_seq, 0))
    widths = (RET_QK, RET_QK, RET_V, RET_V, SSM_WIDTH, D_MODEL, D_MODEL)
    return pl.pallas_call(
        _in_proj_kernel,
        grid=(t // TOKEN_TILE,),
        in_specs=[tile(D_MODEL), _resident(w_in.shape), _resident((1, 2 * D_MODEL)), table, table],
        out_specs=[tile(w) for w in widths],
        out_shape=[jax.ShapeDtypeStruct((t, w), BF16) for w in widths],
        compiler_params=_params("parallel"),
        name="in_proj",
    )(x, w_in, b_gate.reshape(1, 2 * D_MODEL), cos_t, sin_t)


def _rotary_tables(seq_len):
    half = RET_QK_DIM // 2
    inv_freq = ROPE_BASE ** (-jnp.arange(half, dtype=F32) / half)
    ang = jnp.arange(seq_len, dtype=F32)[:, None] * inv_freq[None, :]
    cos = jnp.cos(ang)
    sin = jnp.sin(ang)
    cos_t = jnp.tile(jnp.concatenate([cos, cos], axis=-1), (1, RET_HEADS))
    sin_t = jnp.tile(jnp.concatenate([-sin, sin], axis=-1), (1, RET_HEADS))
    return cos_t, sin_t


def _retention_tables():
    c = RET_CHUNK
    log_gamma = jnp.log1p(-jnp.exp2(-5.0 - jnp.arange(RET_HEADS, dtype=F32)))
    idx = jnp.arange(c, dtype=F32)
    dist = jnp.abs(idx[:, None] - idx[None, :])
    decay = jnp.exp(log_gamma[:, None, None] * dist[None])
    expo = jnp.stack([idx + 1.0, c - idx, c - 1.0 - idx, idx], axis=0)
    edge = jnp.exp(log_gamma[:, None, None] * expo[None])
    edge = jnp.broadcast_to(edge[..., None], (RET_HEADS, 4, c, LANES))
    return decay, edge


def _retention_kernel(q_ref, k_ref, v_ref, g_ref, decay_ref, edge_ref, o_ref, kvf_ref, kvb_ref):
    c = RET_CHUNK
    nc = q_ref.shape[1] // c
    lane = lax.broadcasted_iota(jnp.int32, (1, LANES), 1)
    contract_rows = (((0,), (0,)), ((), ()))
    contract_lanes = (((1,), (1,)), ((), ()))
    for e in range(2):
        mine = (lane < RET_QK_DIM) if e == 0 else (lane >= RET_QK_DIM)
        vsl = slice(e * RET_V_DIM, (e + 1) * RET_V_DIM)
        xi_f, xi_b = edge_ref[e, 0], edge_ref[e, 1]
        zeta_f, zeta_b = edge_ref[e, 2], edge_ref[e, 3]
        chunk_decay = xi_f[c - 1:c, :]

        def k_masked(n):
            kc = k_ref[0, n * c:(n + 1) * c, :]
            return jnp.where(mine, kc, jnp.zeros_like(kc))

        state = jnp.zeros((LANES, RET_V_DIM), F32)
        for n in range(nc):
            kvf_ref[n] = state
            if n + 1 < nc:
                kz = (k_masked(n).astype(F32) * zeta_f).astype(BF16)
                kv = lax.dot_general(kz, v_ref[0, n * c:(n + 1) * c, vsl], contract_rows,
                                     preferred_element_type=F32)
                state = chunk_decay * state + kv
        state = jnp.zeros((LANES, RET_V_DIM), F32)
        for n in range(nc - 1, -1, -1):
            kvb_ref[n] = state
            if n > 0:
                kz = (k_masked(n).astype(F32) * zeta_b).astype(BF16)
                kv = lax.dot_general(kz, v_ref[0, n * c:(n + 1) * c, vsl], contract_rows,
                                     preferred_element_type=F32)
                state = chunk_decay * state + kv

        for n in range(nc):
            rows = slice(n * c, (n + 1) * c)
            qc = q_ref[0, rows, :]
            vc = v_ref[0, rows, vsl]
            scores = lax.dot_general(qc, k_masked(n), contract_lanes, preferred_element_type=F32)
            scores = scores * decay_ref[e]
            o = _dot(scores.astype(BF16), vc)
            qf = qc.astype(F32)
            o = o + _dot((qf * xi_f).astype(BF16), kvf_ref[n].astype(BF16))
            o = o + _dot((qf * xi_b).astype(BF16), kvb_ref[n].astype(BF16))
            mu = jnp.mean(o, axis=-1, keepdims=True)
            oc = o - mu
            var = jnp.mean(oc * oc, axis=-1, keepdims=True)
            on = oc * lax.rsqrt(var + LN_EPS)
            gate = g_ref[0, rows, vsl].astype(F32)
            o_ref[0, rows, vsl] = (gate * _sigmoid(gate) * on).astype(BF16)


def _retention(q, k, v, g, decay, edge):
    b, seq_len, _ = q.shape
    nc = seq_len // RET_CHUNK
    qk_spec = pl.BlockSpec((1, seq_len, LANES), lambda i, h: (i, 0, h))
    v_spec = pl.BlockSpec((1, seq_len, 2 * RET_V_DIM), lambda i, h: (i, 0, h))
    return pl.pallas_call(
        _retention_kernel,
        grid=(b, RET_HEADS // 2),
        in_specs=[qk_spec, qk_spec, v_spec, v_spec,
                  pl.BlockSpec((2, RET_CHUNK, RET_CHUNK), lambda i, h: (h, 0, 0)),
                  pl.BlockSpec((2, 4, RET_CHUNK, LANES), lambda i, h: (h, 0, 0, 0))],
        out_specs=v_spec,
        out_shape=jax.ShapeDtypeStruct((b, seq_len, RET_V), BF16),
        scratch_shapes=[pltpu.VMEM((nc, LANES, RET_V_DIM), F32), pltpu.VMEM((nc, LANES, RET_V_DIM), F32)],
        compiler_params=_params("parallel", "parallel"),
        name="retention",
    )(q, k, v, g, decay, edge)


def _s5_matrices(lam_re, lam_im, log_dt, b_re, b_im, c_re, c_im, d_skip):
    hi = lax.Precision.HIGHEST
    cs, n, p = S5_CHUNK, SSM_STATE, SSM_GROUP
    lr = jnp.minimum(lam_re.astype(F32), EIG_CLIP)
    li = lam_im.astype(F32)
    dt = jnp.exp(log_dt.astype(F32))[..., None]
    groups, width = SSM_GROUPS, cs * p
    steps = jnp.arange(cs + 1, dtype=F32)[None, None, :, None]
    mag = jnp.exp((lr * dt)[:, :, None, :] * steps)
    ang = (li * dt)[:, :, None, :] * steps
    pw_re = mag * jnp.cos(ang)
    pw_im = mag * jnp.sin(ang)
    ab_re, ab_im = pw_re[:, :, 1], pw_im[:, :, 1]
    den = lr * lr + li * li
    num_re = ab_re - 1.0
    f_re = ((num_re * lr + ab_im * li) / den)[:, :, None, :]
    f_im = ((ab_im * lr - num_re * li) / den)[:, :, None, :]
    br_t = b_re.astype(F32).transpose(0, 1, 3, 2)
    bi_t = b_im.astype(F32).transpose(0, 1, 3, 2)
    bb_re = f_re * br_t - f_im * bi_t
    bb_im = f_re * bi_t + f_im * br_t
    cr_t = c_re.astype(F32).transpose(0, 1, 3, 2)[:, :, :, None, :]
    ci_t = c_im.astype(F32).transpose(0, 1, 3, 2)[:, :, :, None, :]
    pwn_re = pw_re.transpose(0, 1, 3, 2)[..., None]
    pwn_im = pw_im.transpose(0, 1, 3, 2)[..., None]
    cp_re = cr_t * pwn_re - ci_t * pwn_im
    cp_im = cr_t * pwn_im + ci_t * pwn_re

    lhs = jnp.concatenate([bb_re, -bb_im], axis=-1)
    rhs = jnp.concatenate([cp_re[:, :, :, :cs], cp_im[:, :, :, :cs]], axis=2).reshape(2, groups, 2 * n, width)
    taps = jnp.einsum('dgin,dgnl->dgil', lhs, rhs, precision=hi)
    zeros = jnp.zeros((groups, p, width), F32)
    pad_f = jnp.concatenate([zeros, taps[0]], axis=-1)
    pad_b = jnp.concatenate([taps[1].reshape(groups, p, cs, p)[:, :, ::-1].reshape(groups, p, width), zeros], axis=-1)
    toep = jnp.stack([pad_f[:, :, width - p * s:2 * width - p * s]
                      + pad_b[:, :, p * (cs - 1 - s):p * (cs - 1 - s) + width] for s in range(cs)], axis=1)
    toep = toep.reshape(groups, width, width)

    def state_cols(pw_r, pw_i, bbr, bbi):
        pw_r, pw_i, bbr, bbi = pw_r[:, :, None, :], pw_i[:, :, None, :], bbr[:, None], bbi[:, None]
        shape = (groups, width, n)
        return (pw_r * bbr - pw_i * bbi).reshape(shape), (pw_r * bbi + pw_i * bbr).reshape(shape)

    sf_re, sf_im = state_cols(pw_re[0, :, :cs][:, ::-1], pw_im[0, :, :cs][:, ::-1], bb_re[0], bb_im[0])
    sb_re, sb_im = state_cols(pw_re[1, :, :cs], pw_im[1, :, :cs], bb_re[1], bb_im[1])
    w_state = jnp.concatenate([sf_re, sb_re, sf_im, sb_im], axis=-1)

    rows = lambda a: a.reshape(groups, n, width)
    w_out = jnp.concatenate([rows(cp_re[0, :, :, 1:]), rows(cp_re[1, :, :, 1:][:, :, ::-1]),
                             rows(-cp_im[0, :, :, 1:]), rows(-cp_im[1, :, :, 1:][:, :, ::-1])], axis=1)

    a_re = jnp.concatenate([pw_re[0, :, cs], pw_re[1, :, cs]], axis=-1)
    a_im = jnp.concatenate([pw_im[0, :, cs], pw_im[1, :, cs]], axis=-1)
    a_pow = jnp.stack([a_re, a_im], axis=1)
    d_lanes = jnp.tile(d_skip.astype(F32).reshape(SSM_GROUPS, 1, p), (1, 1, cs))
    return toep.astype(BF16), w_state.astype(BF16), w_out.astype(BF16), a_pow, d_lanes


def _s5_kernel(u_ref, toep_ref, wst_ref, wout_ref, apow_ref, d_ref, z_ref,
               loc_ref, fwd_re_ref, fwd_im_ref, bwd_re_ref, bwd_im_ref):
    _, nch, tb, width = u_ref.shape
    u = u_ref[0].reshape(nch * tb, width)
    loc_ref[...] = _dot(u, wst_ref[0]).reshape(nch, tb, width)

    half = width // 2
    lane = lax.broadcasted_iota(jnp.int32, (1, half), 1)
    is_fwd = lane < SSM_STATE
    a_re = apow_ref[0, 0:1, :]
    a_im = apow_ref[0, 1:2, :]

    def step(i, carry):
        h_re, h_im = carry
        j = nch - 1 - i
        fwd_re_ref[i] = h_re
        fwd_im_ref[i] = h_im
        bwd_re_ref[j] = h_re
        bwd_im_ref[j] = h_im
        s_re = jnp.where(is_fwd, loc_ref[i, :, :half], loc_ref[j, :, :half])
        s_im = jnp.where(is_fwd, loc_ref[i, :, half:], loc_ref[j, :, half:])
        return (a_re * h_re - a_im * h_im + s_re, a_re * h_im + a_im * h_re + s_im)

    zero = jnp.zeros((tb, half), F32)
    lax.fori_loop(0, nch, step, (zero, zero))

    st_re = jnp.where(is_fwd, fwd_re_ref[...], bwd_re_ref[...]).reshape(nch * tb, half)
    st_im = jnp.where(is_fwd, fwd_im_ref[...], bwd_im_ref[...]).reshape(nch * tb, half)
    state = jnp.concatenate([st_re, st_im], axis=-1).astype(BF16)
    y = _dot(u, toep_ref[0]) + _dot(state, wout_ref[0]) + u.astype(F32) * d_ref[0]
    gelu = 0.5 * y * (1.0 + jnp.tanh(math.sqrt(2.0 / math.pi) * (y + 0.044715 * (y * y * y))))
    z_ref[0] = gelu.astype(BF16).reshape(nch, tb, width)


def _s5(u_g, toep, w_state, w_out, a_pow, d_lanes):
    groups, nch, b, width = u_g.shape
    tb = S5_BATCH_TILE if b % S5_BATCH_TILE == 0 else b
    io_spec = pl.BlockSpec((1, nch, tb, width), lambda g, i: (g, 0, i, 0))
    mat_spec = pl.BlockSpec((1, width, width), lambda g, i: (g, 0, 0))
    half = width // 2
    return pl.pallas_call(
        _s5_kernel,
        grid=(groups, b // tb),
        in_specs=[io_spec, mat_spec, mat_spec, mat_spec,
                  pl.BlockSpec((1, 2, half), lambda g, i: (g, 0, 0)),
                  pl.BlockSpec((1, 1, width), lambda g, i: (g, 0, 0))],
        out_specs=io_spec,
        out_shape=jax.ShapeDtypeStruct(u_g.shape, BF16),
        scratch_shapes=[pltpu.VMEM((nch, tb, width), F32)] + [pltpu.VMEM((nch, tb, half), F32)] * 4,
        compiler_params=_params("parallel", "parallel"),
        name="s5",
    )(u_g, toep, w_state, w_out, a_pow, d_lanes)


GROUPS_PER_BLOCK = LANES // SSM_GROUP
RUN_STEPS = LANES // SSM_GROUP
RUNS = S5_CHUNK // RUN_STEPS
REGROUP_WIDTH = RUN_STEPS * LANES
REGROUP_TOKENS = 512
STAGE_PAD = 8


def _selection_matrix():
    row = jnp.arange(REGROUP_WIDTH)
    step, rem = row // LANES, row % LANES
    col = (rem // SSM_GROUP) * LANES + step * SSM_GROUP + rem % SSM_GROUP
    return (col[:, None] == jnp.arange(REGROUP_WIDTH)[None, :]).astype(BF16)


def _to_groups_kernel(u_ref, sel_ref, o_ref, stage_ref, xcat_ref):
    tb, lc, _ = u_ref.shape
    nr = lc // S5_CHUNK
    slab = lc + STAGE_PAD
    for b in range(tb):
        stage_ref[b * slab:b * slab + lc, :] = u_ref[b].astype(F32)
    for r in range(nr):
        for s in range(S5_CHUNK):
            piece = stage_ref[pl.ds(r * S5_CHUNK + s, tb, stride=slab), :]
            run, step = divmod(s, RUN_STEPS)
            xcat_ref[run, r * tb:(r + 1) * tb, step * LANES:(step + 1) * LANES] = piece.astype(BF16)
    for run in range(RUNS):
        grouped = _dot(xcat_ref[run], sel_ref[...]).astype(BF16)
        for gam in range(GROUPS_PER_BLOCK):
            o_ref[gam, :, :, run * LANES:(run + 1) * LANES] = (
                grouped[:, gam * LANES:(gam + 1) * LANES].reshape(nr, tb, LANES))


def _to_tokens_kernel(z_ref, selt_ref, o_ref, stage_ref):
    _, nr, tb, _ = z_ref.shape
    lc = nr * S5_CHUNK
    slab = lc + STAGE_PAD
    for run in range(RUNS):
        zcat = jnp.concatenate([z_ref[gam, :, :, run * LANES:(run + 1) * LANES].reshape(nr * tb, LANES)
                                for gam in range(GROUPS_PER_BLOCK)], axis=-1)
        zt = _dot(zcat, selt_ref[...])
        for r in range(nr):
            for step in range(RUN_STEPS):
                t = run * RUN_STEPS + step
                stage_ref[pl.ds(r * S5_CHUNK + t, tb, stride=slab), :] = (
                    zt[r * tb:(r + 1) * tb, step * LANES:(step + 1) * LANES])
    for b in range(tb):
        o_ref[b] = stage_ref[b * slab:b * slab + lc, :].astype(BF16)


def _regroup_specs(b, seq_len):
    tb = S5_BATCH_TILE if b % S5_BATCH_TILE == 0 else b
    lc = REGROUP_TOKENS
    grid = (b // tb, seq_len // lc, SSM_WIDTH // LANES)
    token_spec = pl.BlockSpec((tb, lc, LANES), lambda i, c, j: (i, c, j))
    group_spec = pl.BlockSpec((GROUPS_PER_BLOCK, lc // S5_CHUNK, tb, S5_CHUNK * SSM_GROUP),
                              lambda i, c, j: (j, c, i, 0))
    stage = pltpu.VMEM((tb * (lc + STAGE_PAD), LANES), F32)
    return tb, grid, token_spec, group_spec, stage


def _to_group_major(u, sel, b, seq_len):
    tb, grid, token_spec, group_spec, stage = _regroup_specs(b, seq_len)
    return pl.pallas_call(
        _to_groups_kernel,
        grid=grid,
        in_specs=[token_spec, _resident(sel.shape)],
        out_specs=group_spec,
        out_shape=jax.ShapeDtypeStruct((SSM_GROUPS, seq_len // S5_CHUNK, b, S5_CHUNK * SSM_GROUP), BF16),
        scratch_shapes=[stage, pltpu.VMEM((RUNS, REGROUP_TOKENS // S5_CHUNK * tb, REGROUP_WIDTH), BF16)],
        compiler_params=_params("parallel", "parallel", "parallel"),
        name="to_groups",
    )(u, sel)


def _to_token_major(z_g, sel_t, b, seq_len):
    tb, grid, token_spec, group_spec, stage = _regroup_specs(b, seq_len)
    return pl.pallas_call(
        _to_tokens_kernel,
        grid=grid,
        in_specs=[group_spec, _resident(sel_t.shape)],
        out_specs=token_spec,
        out_shape=jax.ShapeDtypeStruct((b, seq_len, SSM_WIDTH), BF16),
        scratch_shapes=[stage],
        compiler_params=_params("parallel", "parallel", "parallel"),
        name="to_tokens",
    )(z_g, sel_t)


def _merge_ln_kernel(x_ref, oret_ref, z_ref, gr_ref, gs_ref, wo_ref, wglu_ref, wout_ref, g_ref, b_ref, o_ref):
    oret = oret_ref[...]
    z = z_ref[...]
    pieces = []
    for lo in range(0, D_MODEL, MERGE_COLS):
        cols = slice(lo, lo + MERGE_COLS)
        y_ret = _dot(oret, wo_ref[:, cols])
        val = _dot(z, wglu_ref[:, cols])
        gate = _dot(z, wglu_ref[:, D_MODEL + lo:D_MODEL + lo + MERGE_COLS])
        y_ssm = val * _sigmoid(gate)
        pieces.append((gr_ref[:, cols].astype(F32) * y_ret + gs_ref[:, cols].astype(F32) * y_ssm).astype(BF16))
    mix = _dot(jnp.concatenate(pieces, axis=-1), wout_ref[...])
    o_ref[...] = _layer_norm(ALPHA * x_ref[...] + mix, g_ref[...], b_ref[...])


def _merge_ln(x, oret, z, gr, gs, w_o, w_glu, w_out, ln_g, ln_b):
    t = x.shape[0]
    tile = pl.BlockSpec((TOKEN_TILE, D_MODEL), lambda i: (i, 0))
    return pl.pallas_call(
        _merge_ln_kernel,
        grid=(t // TOKEN_TILE,),
        in_specs=[tile] * 5 + [_resident(w_o.shape), _resident(w_glu.shape), _resident(w_out.shape),
                               _resident((1, D_MODEL)), _resident((1, D_MODEL))],
        out_specs=tile,
        out_shape=jax.ShapeDtypeStruct((t, D_MODEL), F32),
        compiler_params=_params("parallel"),
        name="merge_ln",
    )(x, oret, z, gr, gs, w_o, w_glu, w_out, ln_g.reshape(1, D_MODEL), ln_b.reshape(1, D_MODEL))


def _trunk(xs, params):
    (ffn1_w_gu, ffn1_w_down, ln1_g, ln1_b, w_in, b_gate, ret_w_o, s5_lam_re, s5_lam_im, s5_log_dt,
     s5_b_re, s5_b_im, s5_c_re, s5_c_im, s5_d, s5_w_glu, w_out, ln2_g, ln2_b,
     ffn2_w_gu, ffn2_w_down, ln3_g, ln3_b) = params
    seq_len = xs[0].shape[1]
    assert all(x.shape[1:] == (seq_len, D_MODEL) for x in xs)
    assert seq_len % TOKEN_TILE == 0 and seq_len % RET_CHUNK == 0 and seq_len % REGROUP_TOKENS == 0
    sizes = [x.shape[0] * seq_len for x in xs]
    starts = [sum(sizes[:i]) for i in range(len(xs))]
    b, t = sum(x.shape[0] for x in xs), sum(sizes)
    cos_t, sin_t = _rotary_tables(seq_len)
    decay, edge = _retention_tables()
    sel = _selection_matrix()
    bf = lambda w: w.astype(BF16)
    x = None
    for l in range(DEPTH):
        ffn1 = (bf(ffn1_w_gu[l]), bf(ffn1_w_down[l]), ln1_g[l], ln1_b[l])
        if l == 0:
            for xi, size, start in zip(xs, sizes, starts):
                x = _ffn_ln(xi.reshape(size, D_MODEL), *ffn1, out_rows=t, out_row0=start, into=x)
        else:
            x = _ffn_ln(x, *ffn1)
        q, k, v, g, u, gr, gs = _in_proj(x, bf(w_in[l]), b_gate[l], cos_t, sin_t, seq_len)
        shape3 = lambda a: a.reshape(b, seq_len, a.shape[-1])
        oret = _retention(shape3(q), shape3(k), shape3(v), shape3(g), decay, edge)
        mats = _s5_matrices(s5_lam_re[l], s5_lam_im[l], s5_log_dt[l], s5_b_re[l], s5_b_im[l],
                            s5_c_re[l], s5_c_im[l], s5_d[l])
        z = _to_token_major(_s5(_to_group_major(shape3(u), sel, b, seq_len), *mats), sel.T, b, seq_len)
        x = _merge_ln(x, oret.reshape(t, RET_V), z.reshape(t, SSM_WIDTH), gr, gs,
                      bf(ret_w_o[l]), bf(s5_w_glu[l]), bf(w_out[l]), ln2_g[l], ln2_b[l])
        ffn2 = (bf(ffn2_w_gu[l]), bf(ffn2_w_down[l]), ln3_g[l], ln3_b[l])
        if l < DEPTH - 1:
            x = _ffn_ln(x, *ffn2)
        else:
            ys = [_ffn_ln(x, *ffn2, in_row0=start, rows=size) for size, start in zip(sizes, starts)]
    return [y.reshape(xi.shape) for y, xi in zip(ys, xs)]


def kernel(x_prompt, x_sample, ffn1_w_gu, ffn1_w_down, ln1_g, ln1_b, w_in, b_gate, ret_w_o, s5_lam_re, s5_lam_im, s5_log_dt, s5_b_re, s5_b_im, s5_c_re, s5_c_im, s5_d, s5_w_glu, w_out, ln2_g, ln2_b, ffn2_w_gu, ffn2_w_down, ln3_g, ln3_b):
    params = (ffn1_w_gu, ffn1_w_down, ln1_g, ln1_b, w_in, b_gate, ret_w_o, s5_lam_re, s5_lam_im, s5_log_dt,
              s5_b_re, s5_b_im, s5_c_re, s5_c_im, s5_d, s5_w_glu, w_out, ln2_g, ln2_b,
              ffn2_w_gu, ffn2_w_down, ln3_g, ln3_b)
    y_prompt, y_sample = _trunk([x_prompt, x_sample], params)
    return (y_prompt, y_sample)
```

```python
import functools
import math

import jax
import jax.numpy as jnp
from jax import lax
from jax.experimental import pallas as pl
from jax.experimental.pallas import tpu as pltpu

F32 = jnp.float32
BF16 = jnp.bfloat16

D_MODEL = 1024
DEPTH = 2
RET_HEADS = 8
RET_QK_DIM = 64
RET_V_DIM = 128
RET_QK = RET_HEADS * RET_QK_DIM
RET_V = RET_HEADS * RET_V_DIM
ROPE_BASE = 10000.0
SSM_WIDTH = 1024
SSM_GROUP = 16
SSM_GROUPS = SSM_WIDTH // SSM_GROUP
SSM_STATE = 64
EIG_CLIP = -1e-4
D_FF = 2816
ALPHA = (2 * DEPTH) ** 0.25
LN_EPS = 1e-5

LANES = 128
MXU_DIM = 256
VMEM_LIMIT_BYTES = 56 * 1024 * 1024

TOKEN_TILE = 512
FF_CHUNKS = (1024, 1024, 768)
assert sum(FF_CHUNKS) == D_FF and all(w % MXU_DIM == 0 for w in FF_CHUNKS)
MERGE_COLS = 512
RET_CHUNK = MXU_DIM
S5_CHUNK = MXU_DIM // SSM_GROUP
S5_BATCH_TILE = 16


def _resident(shape):
    nd = len(shape)
    return pl.BlockSpec(shape, lambda *_: (0,) * nd, pipeline_mode=pl.Buffered(1))


def _params(*sem):
    return pltpu.CompilerParams(dimension_semantics=sem, vmem_limit_bytes=VMEM_LIMIT_BYTES)


def _dot(a, b):
    return jnp.dot(a, b, preferred_element_type=F32)


def _sigmoid(x):
    return 1.0 / (1.0 + jnp.exp(-x))


def _layer_norm(y, g, b):
    mu = jnp.mean(y, axis=-1, keepdims=True)
    yc = y - mu
    var = jnp.mean(yc * yc, axis=-1, keepdims=True)
    return yc * lax.rsqrt(var + LN_EPS) * g + b


def _delayed_layer_norm(pre_ref, g_ref, b_ref, o_ref):
    @pl.when(pl.program_id(0) == 0)
    def _():
        pre_ref[...] = jnp.zeros_like(pre_ref)

    o_ref[...] = _layer_norm(pre_ref[...], g_ref[...], b_ref[...])


def _delayed_specs(rows, in_t0, out_t0, width):
    n = rows // TOKEN_TILE
    in_map = lambda i: (jnp.minimum(i, n - 1) + in_t0, 0)
    out_map = lambda i: (jnp.maximum(i - 1, 0) + out_t0, 0)
    return (n + 1,), in_map, pl.BlockSpec((TOKEN_TILE, width), out_map)


def _ffn_ln_kernel(x_ref, wgu_ref, wdn_ref, g_ref, b_ref, *rest):
    o_ref, pre_ref = rest[-2:]
    _delayed_layer_norm(pre_ref, g_ref, b_ref, o_ref)
    x = x_ref[...]
    xb = x.astype(BF16)
    acc = jnp.zeros(x.shape, F32)
    lo = 0
    for width in FF_CHUNKS:
        a = _dot(xb, wgu_ref[:, lo:lo + width])
        u = _dot(xb, wgu_ref[:, D_FF + lo:D_FF + lo + width])
        h = (a * _sigmoid(a) * u).astype(BF16)
        acc = acc + _dot(h, wdn_ref[lo:lo + width, :])
        lo += width
    pre_ref[...] = ALPHA * x + 0.5 * acc


def _ffn_ln(x, w_gu, w_down, ln_g, ln_b, *, in_row0=0, rows=None, out_rows=None, out_row0=0, into=None):
    rows = x.shape[0] if rows is None else rows
    out_rows = rows if out_rows is None else out_rows
    grid, in_map, out_spec = _delayed_specs(rows, in_row0 // TOKEN_TILE, out_row0 // TOKEN_TILE, D_MODEL)
    args = [x, w_gu, w_down, ln_g.reshape(1, D_MODEL), ln_b.reshape(1, D_MODEL)]
    in_specs = [pl.BlockSpec((TOKEN_TILE, D_MODEL), in_map),
                _resident(w_gu.shape), _resident(w_down.shape),
                _resident((1, D_MODEL)), _resident((1, D_MODEL))]
    aliases = {}
    if into is not None:
        args.append(into)
        in_specs.append(pl.BlockSpec(memory_space=pl.ANY))
        aliases = {len(args) - 1: 0}
        out_rows = into.shape[0]
    return pl.pallas_call(
        _ffn_ln_kernel,
        grid=grid,
        in_specs=in_specs,
        out_specs=out_spec,
        out_shape=jax.ShapeDtypeStruct((out_rows, D_MODEL), F32),
        input_output_aliases=aliases,
        scratch_shapes=[pltpu.VMEM((TOKEN_TILE, D_MODEL), F32)],
        compiler_params=_params("arbitrary"),
        name="ffn_ln",
    )(*args)


def _in_proj_kernel(x_ref, w_ref, bg_ref, cos_ref, sin_ref,
                    q_ref, k_ref, v_ref, g_ref, u_ref, gr_ref, gs_ref):
    xb = x_ref[...].astype(BF16)

    def seg(lo, width):
        return _dot(xb, w_ref[:, lo:lo + width])

    lane = lax.broadcasted_iota(jnp.int32, (1, RET_QK), 1)
    first_half = (lane % RET_QK_DIM) < (RET_QK_DIM // 2)
    cos = cos_ref[...]
    sin = sin_ref[...]

    def rotary(t):
        half = RET_QK_DIM // 2
        swapped = jnp.where(first_half, pltpu.roll(t, RET_QK - half, axis=1), pltpu.roll(t, half, axis=1))
        return t * cos + swapped * sin

    base = 2 * RET_QK
    gate0 = base + 2 * RET_V + SSM_WIDTH
    gr_ref[...] = _sigmoid(seg(gate0, D_MODEL) + bg_ref[:, :D_MODEL]).astype(BF16)
    gs_ref[...] = _sigmoid(seg(gate0 + D_MODEL, D_MODEL) + bg_ref[:, D_MODEL:]).astype(BF16)
    q_ref[...] = rotary(seg(0, RET_QK)).astype(BF16)
    k_ref[...] = (rotary(seg(RET_QK, RET_QK)) * (RET_QK_DIM ** -0.5)).astype(BF16)
    v_ref[...] = seg(base, RET_V).astype(BF16)
    g_ref[...] = seg(base + RET_V, RET_V).astype(BF16)
    u_ref[...] = seg(base + 2 * RET_V, SSM_WIDTH).astype(BF16)


def _in_proj(x, w_in, b_gate, cos_t, sin_t, seq_len):
    t = x.shape[0]
    tiles_per_seq = seq_len // TOKEN_TILE
    tile = lambda w: pl.BlockSpec((TOKEN_TILE, w), lambda i: (i, 0))
    table = pl.BlockSpec((TOKEN_TILE, RET_QK), lambda i: (i % tiles_per_seq, 0))
    widths = (RET_QK, RET_QK, RET_V, RET_V, SSM_WIDTH, D_MODEL, D_MODEL)
    return pl.pallas_call(
        _in_proj_kernel,
        grid=(t // TOKEN_TILE,),
        in_specs=[tile(D_MODEL), _resident(w_in.shape), _resident((1, 2 * D_MODEL)), table, table],
        out_specs=[tile(w) for w in widths],
        out_shape=[jax.ShapeDtypeStruct((t, w), BF16) for w in widths],
        compiler_params=_params("parallel"),
        name="in_proj",
    )(x, w_in, b_gate.reshape(1, 2 * D_MODEL), cos_t, sin_t)


def _rotary_tables(seq_len):
    half = RET_QK_DIM // 2
    inv_freq = ROPE_BASE ** (-jnp.arange(half, dtype=F32) / half)
    ang = jnp.arange(seq_len, dtype=F32)[:, None] * inv_freq[None, :]
    cos = jnp.cos(ang)
    sin = jnp.sin(ang)
    cos_t = jnp.tile(jnp.concatenate([cos, cos], axis=-1), (1, RET_HEADS))
    sin_t = jnp.tile(jnp.concatenate([-sin, sin], axis=-1), (1, RET_HEADS))
    return cos_t, sin_t


def _retention_tables():
    c = RET_CHUNK
    log_gamma = jnp.log1p(-jnp.exp2(-5.0 - jnp.arange(RET_HEADS, dtype=F32)))
    idx = jnp.arange(c, dtype=F32)
    dist = jnp.abs(idx[:, None] - idx[None, :])
    decay = jnp.exp(log_gamma[:, None, None] * dist[None])
    expo = jnp.stack([idx + 1.0, c - idx, c - 1.0 - idx, idx], axis=0)
    edge = jnp.exp(log_gamma[:, None, None] * expo[None])
    edge = jnp.broadcast_to(edge[..., None], (RET_HEADS, 4, c, LANES))
    return decay, edge


def _retention_kernel(q_ref, k_ref, v_ref, g_ref, decay_ref, edge_ref, o_ref, kvf_ref, kvb_ref):
    c = RET_CHUNK
    nc = q_ref.shape[1] // c
    lane = lax.broadcasted_iota(jnp.int32, (1, LANES), 1)
    contract_rows = (((0,), (0,)), ((), ()))
    contract_lanes = (((1,), (1,)), ((), ()))
    for e in range(2):
        mine = (lane < RET_QK_DIM) if e == 0 else (lane >= RET_QK_DIM)
        vsl = slice(e * RET_V_DIM, (e + 1) * RET_V_DIM)
        xi_f, xi_b = edge_ref[e, 0], edge_ref[e, 1]
        zeta_f, zeta_b = edge_ref[e, 2], edge_ref[e, 3]
        chunk_decay = xi_f[c - 1:c, :]

        def k_masked(n):
            kc = k_ref[0, n * c:(n + 1) * c, :]
            return jnp.where(mine, kc, jnp.zeros_like(kc))

        state = jnp.zeros((LANES, RET_V_DIM), F32)
        for n in range(nc):
            kvf_ref[n] = state
            if n + 1 < nc:
                kz = (k_masked(n).astype(F32) * zeta_f).astype(BF16)
                kv = lax.dot_general(kz, v_ref[0, n * c:(n + 1) * c, vsl], contract_rows,
                                     preferred_element_type=F32)
                state = chunk_decay * state + kv
        state = jnp.zeros((LANES, RET_V_DIM), F32)
        for n in range(nc - 1, -1, -1):
            kvb_ref[n] = state
            if n > 0:
                kz = (k_masked(n).astype(F32) * zeta_b).astype(BF16)
                kv = lax.dot_general(kz, v_ref[0, n * c:(n + 1) * c, vsl], contract_rows,
                                     preferred_element_type=F32)
                state = chunk_decay * state + kv

        for n in range(nc):
            rows = slice(n * c, (n + 1) * c)
            qc = q_ref[0, rows, :]
            vc = v_ref[0, rows, vsl]
            scores = lax.dot_general(qc, k_masked(n), contract_lanes, preferred_element_type=F32)
            scores = scores * decay_ref[e]
            o = _dot(scores.astype(BF16), vc)
            qf = qc.astype(F32)
            o = o + _dot((qf * xi_f).astype(BF16), kvf_ref[n].astype(BF16))
            o = o + _dot((qf * xi_b).astype(BF16), kvb_ref[n].astype(BF16))
            mu = jnp.mean(o, axis=-1, keepdims=True)
            oc = o - mu
            var = jnp.mean(oc * oc, axis=-1, keepdims=True)
            on = oc * lax.rsqrt(var + LN_EPS)
            gate = g_ref[0, rows, vsl].astype(F32)
            o_ref[0, rows, vsl] = (gate * _sigmoid(gate) * on).astype(BF16)


def _retention(q, k, v, g, decay, edge):
    b, seq_len, _ = q.shape
    nc = seq_len // RET_CHUNK
    qk_spec = pl.BlockSpec((1, seq_len, LANES), lambda i, h: (i, 0, h))
    v_spec = pl.BlockSpec((1, seq_len, 2 * RET_V_DIM), lambda i, h: (i, 0, h))
    return pl.pallas_call(
        _retention_kernel,
        grid=(b, RET_HEADS // 2),
        in_specs=[qk_spec, qk_spec, v_spec, v_spec,
                  pl.BlockSpec((2, RET_CHUNK, RET_CHUNK), lambda i, h: (h, 0, 0)),
                  pl.BlockSpec((2, 4, RET_CHUNK, LANES), lambda i, h: (h, 0, 0, 0))],
        out_specs=v_spec,
        out_shape=jax.ShapeDtypeStruct((b, seq_len, RET_V), BF16),
        scratch_shapes=[pltpu.VMEM((nc, LANES, RET_V_DIM), F32), pltpu.VMEM((nc, LANES, RET_V_DIM), F32)],
        compiler_params=_params("parallel", "parallel"),
        name="retention",
    )(q, k, v, g, decay, edge)


def _s5_matrices(lam_re, lam_im, log_dt, b_re, b_im, c_re, c_im, d_skip):
    hi = lax.Precision.HIGHEST
    cs, n, p = S5_CHUNK, SSM_STATE, SSM_GROUP
    lr = jnp.minimum(lam_re.astype(F32), EIG_CLIP)
    li = lam_im.astype(F32)
    dt = jnp.exp(log_dt.astype(F32))[..., None]
    groups, width = SSM_GROUPS, cs * p
    steps = jnp.arange(cs + 1, dtype=F32)[None, None, :, None]
    mag = jnp.exp((lr * dt)[:, :, None, :] * steps)
    ang = (li * dt)[:, :, None, :] * steps
    pw_re = mag * jnp.cos(ang)
    pw_im = mag * jnp.sin(ang)
    ab_re, ab_im = pw_re[:, :, 1], pw_im[:, :, 1]
    den = lr * lr + li * li
    num_re = ab_re - 1.0
    f_re = ((num_re * lr + ab_im * li) / den)[:, :, None, :]
    f_im = ((ab_im * lr - num_re * li) / den)[:, :, None, :]
    br_t = b_re.astype(F32).transpose(0, 1, 3, 2)
    bi_t = b_im.astype(F32).transpose(0, 1, 3, 2)
    bb_re = f_re * br_t - f_im * bi_t
    bb_im = f_re * bi_t + f_im * br_t
    cr_t = c_re.astype(F32).transpose(0, 1, 3, 2)[:, :, :, None, :]
    ci_t = c_im.astype(F32).transpose(0, 1, 3, 2)[:, :, :, None, :]
    pwn_re = pw_re.transpose(0, 1, 3, 2)[..., None]
    pwn_im = pw_im.transpose(0, 1, 3, 2)[..., None]
    cp_re = cr_t * pwn_re - ci_t * pwn_im
    cp_im = cr_t * pwn_im + ci_t * pwn_re

    lhs = jnp.concatenate([bb_re, -bb_im], axis=-1)
    rhs = jnp.concatenate([cp_re[:, :, :, :cs], cp_im[:, :, :, :cs]], axis=2).reshape(2, groups, 2 * n, width)
    taps = jnp.einsum('dgin,dgnl->dgil', lhs, rhs, precision=hi)
    zeros = jnp.zeros((groups, p, width), F32)
    pad_f = jnp.concatenate([zeros, taps[0]], axis=-1)
    pad_b = jnp.concatenate([taps[1].reshape(groups, p, cs, p)[:, :, ::-1].reshape(groups, p, width), zeros], axis=-1)
    toep = jnp.stack([pad_f[:, :, width - p * s:2 * width - p * s]
                      + pad_b[:, :, p * (cs - 1 - s):p * (cs - 1 - s) + width] for s in range(cs)], axis=1)
    toep = toep.reshape(groups, width, width)

    def state_cols(pw_r, pw_i, bbr, bbi):
        pw_r, pw_i, bbr, bbi = pw_r[:, :, None, :], pw_i[:, :, None, :], bbr[:, None], bbi[:, None]
        shape = (groups, width, n)
        return (pw_r * bbr - pw_i * bbi).reshape(shape), (pw_r * bbi + pw_i * bbr).reshape(shape)

    sf_re, sf_im = state_cols(pw_re[0, :, :cs][:, ::-1], pw_im[0, :, :cs][:, ::-1], bb_re[0], bb_im[0])
    sb_re, sb_im = state_cols(pw_re[1, :, :cs], pw_im[1, :, :cs], bb_re[1], bb_im[1])
    w_state = jnp.concatenate([sf_re, sb_re, sf_im, sb_im], axis=-1)

    rows = lambda a: a.reshape(groups, n, width)
    w_out = jnp.concatenate([rows(cp_re[0, :, :, 1:]), rows(cp_re[1, :, :, 1:][:, :, ::-1]),
                             rows(-cp_im[0, :, :, 1:]), rows(-cp_im[1, :, :, 1:][:, :, ::-1])], axis=1)

    a_re = jnp.concatenate([pw_re[0, :, cs], pw_re[1, :, cs]], axis=-1)
    a_im = jnp.concatenate([pw_im[0, :, cs], pw_im[1, :, cs]], axis=-1)
    a_pow = jnp.stack([a_re, a_im], axis=1)
    d_lanes = jnp.tile(d_skip.astype(F32).reshape(SSM_GROUPS, 1, p), (1, 1, cs))
    return toep.astype(BF16), w_state.astype(BF16), w_out.astype(BF16), a_pow, d_lanes


def _s5_kernel(u_ref, toep_ref, wst_ref, wout_ref, apow_ref, d_ref, z_ref,
               loc_ref, fwd_re_ref, fwd_im_ref, bwd_re_ref, bwd_im_ref):
    _, nch, tb, width = u_ref.shape
    u = u_ref[0].reshape(nch * tb, width)
    loc_ref[...] = _dot(u, wst_ref[0]).reshape(nch, tb, width)

    half = width // 2
    lane = lax.broadcasted_iota(jnp.int32, (1, half), 1)
    is_fwd = lane < SSM_STATE
    a_re = apow_ref[0, 0:1, :]
    a_im = apow_ref[0, 1:2, :]

    def step(i, carry):
        h_re, h_im = carry
        j = nch - 1 - i
        fwd_re_ref[i] = h_re
        fwd_im_ref[i] = h_im
        bwd_re_ref[j] = h_re
        bwd_im_ref[j] = h_im
        s_re = jnp.where(is_fwd, loc_ref[i, :, :half], loc_ref[j, :, :half])
        s_im = jnp.where(is_fwd, loc_ref[i, :, half:], loc_ref[j, :, half:])
        return (a_re * h_re - a_im * h_im + s_re, a_re * h_im + a_im * h_re + s_im)

    zero = jnp.zeros((tb, half), F32)
    lax.fori_loop(0, nch, step, (zero, zero))

    st_re = jnp.where(is_fwd, fwd_re_ref[...], bwd_re_ref[...]).reshape(nch * tb, half)
    st_im = jnp.where(is_fwd, fwd_im_ref[...], bwd_im_ref[...]).reshape(nch * tb, half)
    state = jnp.concatenate([st_re, st_im], axis=-1).astype(BF16)
    y = _dot(u, toep_ref[0]) + _dot(state, wout_ref[0]) + u.astype(F32) * d_ref[0]
    gelu = 0.5 * y * (1.0 + jnp.tanh(math.sqrt(2.0 / math.pi) * (y + 0.044715 * (y * y * y))))
    z_ref[0] = gelu.astype(BF16).reshape(nch, tb, width)


def _s5(u_g, toep, w_state, w_out, a_pow, d_lanes):
    groups, nch, b, width = u_g.shape
    tb = S5_BATCH_TILE if b % S5_BATCH_TILE == 0 else b
    io_spec = pl.BlockSpec((1, nch, tb, width), lambda g, i: (g, 0, i, 0))
    mat_spec = pl.BlockSpec((1, width, width), lambda g, i: (g, 0, 0))
    half = width // 2
    return pl.pallas_call(
        _s5_kernel,
        grid=(groups, b // tb),
        in_specs=[io_spec, mat_spec, mat_spec, mat_spec,
                  pl.BlockSpec((1, 2, half), lambda g, i: (g, 0, 0)),
                  pl.BlockSpec((1, 1, width), lambda g, i: (g, 0, 0))],
        out_specs=io_spec,
        out_shape=jax.ShapeDtypeStruct(u_g.shape, BF16),
        scratch_shapes=[pltpu.VMEM((nch, tb, width), F32)] + [pltpu.VMEM((nch, tb, half), F32)] * 4,
        compiler_params=_params("parallel", "parallel"),
        name="s5",
    )(u_g, toep, w_state, w_out, a_pow, d_lanes)


GROUPS_PER_BLOCK = LANES // SSM_GROUP
RUN_STEPS = LANES // SSM_GROUP
RUNS = S5_CHUNK // RUN_STEPS
REGROUP_WIDTH = RUN_STEPS * LANES
REGROUP_TOKENS = 512
STAGE_PAD = 8


def _selection_matrix():
    row = jnp.arange(REGROUP_WIDTH)
    step, rem = row // LANES, row % LANES
    col = (rem // SSM_GROUP) * LANES + step * SSM_GROUP + rem % SSM_GROUP
    return (col[:, None] == jnp.arange(REGROUP_WIDTH)[None, :]).astype(BF16)


def _to_groups_kernel(u_ref, sel_ref, o_ref, stage_ref, xcat_ref):
    tb, lc, _ = u_ref.shape
    nr = lc // S5_CHUNK
    slab = lc + STAGE_PAD
    for b in range(tb):
        stage_ref[b * slab:b * slab + lc, :] = u_ref[b].astype(F32)
    for r in range(nr):
        for s in range(S5_CHUNK):
            piece = stage_ref[pl.ds(r * S5_CHUNK + s, tb, stride=slab), :]
            run, step = divmod(s, RUN_STEPS)
            xcat_ref[run, r * tb:(r + 1) * tb, step * LANES:(step + 1) * LANES] = piece.astype(BF16)
    for run in range(RUNS):
        grouped = _dot(xcat_ref[run], sel_ref[...]).astype(BF16)
        for gam in range(GROUPS_PER_BLOCK):
            o_ref[gam, :, :, run * LANES:(run + 1) * LANES] = (
                grouped[:, gam * LANES:(gam + 1) * LANES].reshape(nr, tb, LANES))


def _to_tokens_kernel(z_ref, selt_ref, o_ref, stage_ref):
    _, nr, tb, _ = z_ref.shape
    lc = nr * S5_CHUNK
    slab = lc + STAGE_PAD
    for run in range(RUNS):
        zcat = jnp.concatenate([z_ref[gam, :, :, run * LANES:(run + 1) * LANES].reshape(nr * tb, LANES)
                                for gam in range(GROUPS_PER_BLOCK)], axis=-1)
        zt = _dot(zcat, selt_ref[...])
        for r in range(nr):
            for step in range(RUN_STEPS):
                t = run * RUN_STEPS + step
                stage_ref[pl.ds(r * S5_CHUNK + t, tb, stride=slab), :] = (
                    zt[r * tb:(r + 1) * tb, step * LANES:(step + 1) * LANES])
    for b in range(tb):
        o_ref[b] = stage_ref[b * slab:b * slab + lc, :].astype(BF16)


def _regroup_specs(b, seq_len):
    tb = S5_BATCH_TILE if b % S5_BATCH_TILE == 0 else b
    lc = REGROUP_TOKENS
    grid = (b // tb, seq_len // lc, SSM_WIDTH // LANES)
    token_spec = pl.BlockSpec((tb, lc, LANES), lambda i, c, j: (i, c, j))
    group_spec = pl.BlockSpec((GROUPS_PER_BLOCK, lc // S5_CHUNK, tb, S5_CHUNK * SSM_GROUP),
                              lambda i, c, j: (j, c, i, 0))
    stage = pltpu.VMEM((tb * (lc + STAGE_PAD), LANES), F32)
    return tb, grid, token_spec, group_spec, stage


def _to_group_major(u, sel, b, seq_len):
    tb, grid, token_spec, group_spec, stage = _regroup_specs(b, seq_len)
    return pl.pallas_call(
        _to_groups_kernel,
        grid=grid,
        in_specs=[token_spec, _resident(sel.shape)],
        out_specs=group_spec,
        out_shape=jax.ShapeDtypeStruct((SSM_GROUPS, seq_len // S5_CHUNK, b, S5_CHUNK * SSM_GROUP), BF16),
        scratch_shapes=[stage, pltpu.VMEM((RUNS, REGROUP_TOKENS // S5_CHUNK * tb, REGROUP_WIDTH), BF16)],
        compiler_params=_params("parallel", "parallel", "parallel"),
        name="to_groups",
    )(u, sel)


def _to_token_major(z_g, sel_t, b, seq_len):
    tb, grid, token_spec, group_spec, stage = _regroup_specs(b, seq_len)
    return pl.pallas_call(
        _to_tokens_kernel,
        grid=grid,
        in_specs=[group_spec, _resident(sel_t.shape)],
        out_specs=token_spec,
        out_shape=jax.ShapeDtypeStruct((b, seq_len, SSM_WIDTH), BF16),
        scratch_shapes=[stage],
        compiler_params=_params("parallel", "parallel", "parallel"),
        name="to_tokens",
    )(z_g, sel_t)


def _merge_ln_kernel(x_ref, oret_ref, z_ref, gr_ref, gs_ref, wo_ref, wglu_ref, wout_ref, g_ref, b_ref, o_ref,
                     pre_ref):
    _delayed_layer_norm(pre_ref, g_ref, b_ref, o_ref)
    oret = oret_ref[...]
    z = z_ref[...]
    pieces = []
    for lo in range(0, D_MODEL, MERGE_COLS):
        cols = slice(lo, lo + MERGE_COLS)
        y_ret = _dot(oret, wo_ref[:, cols])
        val = _dot(z, wglu_ref[:, cols])
        gate = _dot(z, wglu_ref[:, D_MODEL + lo:D_MODEL + lo + MERGE_COLS])
        y_ssm = val * _sigmoid(gate)
        pieces.append((gr_ref[:, cols].astype(F32) * y_ret + gs_ref[:, cols].astype(F32) * y_ssm).astype(BF16))
    mix = _dot(jnp.concatenate(pieces, axis=-1), wout_ref[...])
    pre_ref[...] = ALPHA * x_ref[...] + mix


def _merge_ln(x, oret, z, gr, gs, w_o, w_glu, w_out, ln_g, ln_b):
    t = x.shape[0]
    grid, in_map, out_spec = _delayed_specs(t, 0, 0, D_MODEL)
    tile = pl.BlockSpec((TOKEN_TILE, D_MODEL), in_map)
    return pl.pallas_call(
        _merge_ln_kernel,
        grid=grid,
        in_specs=[tile] * 5 + [_resident(w_o.shape), _resident(w_glu.shape), _resident(w_out.shape),
                               _resident((1, D_MODEL)), _resident((1, D_MODEL))],
        out_specs=out_spec,
        out_shape=jax.ShapeDtypeStruct((t, D_MODEL), F32),
        scratch_shapes=[pltpu.VMEM((TOKEN_TILE, D_MODEL), F32)],
        compiler_params=_params("arbitrary"),
        name="merge_ln",
    )(x, oret, z, gr, gs, w_o, w_glu, w_out, ln_g.reshape(1, D_MODEL), ln_b.reshape(1, D_MODEL))


def _trunk(xs, params):
    (ffn1_w_gu, ffn1_w_down, ln1_g, ln1_b, w_in, b_gate, ret_w_o, s5_lam_re, s5_lam_im, s5_log_dt,
     s5_b_re, s5_b_im, s5_c_re, s5_c_im, s5_d, s5_w_glu, w_out, ln2_g, ln2_b,
     ffn2_w_gu, ffn2_w_down, ln3_g, ln3_b) = params
    seq_len = xs[0].shape[1]
    assert all(x.shape[1:] == (seq_len, D_MODEL) for x in xs)
    assert seq_len % TOKEN_TILE == 0 and seq_len % RET_CHUNK == 0 and seq_len % REGROUP_TOKENS == 0
    sizes = [x.shape[0] * seq_len for x in xs]
    starts = [sum(sizes[:i]) for i in range(len(xs))]
    b, t = sum(x.shape[0] for x in xs), sum(sizes)
    cos_t, sin_t = _rotary_tables(seq_len)
    decay, edge = _retention_tables()
    sel = _selection_matrix()
    bf = lambda w: w.astype(BF16)
    x = None
    for l in range(DEPTH):
        ffn1 = (bf(ffn1_w_gu[l]), bf(ffn1_w_down[l]), ln1_g[l], ln1_b[l])
        if l == 0:
            for xi, size, start in zip(xs, sizes, starts):
                x = _ffn_ln(xi.reshape(size, D_MODEL), *ffn1, out_rows=t, out_row0=start, into=x)
        else:
            x = _ffn_ln(x, *ffn1)
        q, k, v, g, u, gr, gs = _in_proj(x, bf(w_in[l]), b_gate[l], cos_t, sin_t, seq_len)
        shape3 = lambda a: a.reshape(b, seq_len, a.shape[-1])
        oret = _retention(shape3(q), shape3(k), shape3(v), shape3(g), decay, edge)
        mats = _s5_matrices(s5_lam_re[l], s5_lam_im[l], s5_log_dt[l], s5_b_re[l], s5_b_im[l],
                            s5_c_re[l], s5_c_im[l], s5_d[l])
        z = _to_token_major(_s5(_to_group_major(shape3(u), sel, b, seq_len), *mats), sel.T, b, seq_len)
        x = _merge_ln(x, oret.reshape(t, RET_V), z.reshape(t, SSM_WIDTH), gr, gs,
                      bf(ret_w_o[l]), bf(s5_w_glu[l]), bf(w_out[l]), ln2_g[l], ln2_b[l])
        ffn2 = (bf(ffn2_w_gu[l]), bf(ffn2_w_down[l]), ln3_g[l], ln3_b[l])
        if l < DEPTH - 1:
            x = _ffn_ln(x, *ffn2)
        else:
            ys = [_ffn_ln(x, *ffn2, in_row0=start, rows=size) for size, start in zip(sizes, starts)]
    return [y.reshape(xi.shape) for y, xi in zip(ys, xs)]


def kernel(x_prompt, x_sample, ffn1_w_gu, ffn1_w_down, ln1_g, ln1_b, w_in, b_gate, ret_w_o, s5_lam_re, s5_lam_im, s5_log_dt, s5_b_re, s5_b_im, s5_c_re, s5_c_im, s5_d, s5_w_glu, w_out, ln2_g, ln2_b, ffn2_w_gu, ffn2_w_down, ln3_g, ln3_b):
    params = (ffn1_w_gu, ffn1_w_down, ln1_g, ln1_b, w_in, b_gate, ret_w_o, s5_lam_re, s5_lam_im, s5_log_dt,
              s5_b_re, s5_b_im, s5_c_re, s5_c_im, s5_d, s5_w_glu, w_out, ln2_g, ln2_b,
              ffn2_w_gu, ffn2_w_down, ln3_g, ln3_b)
    y_prompt, y_sample = _trunk([x_prompt, x_sample], params)
    return (y_prompt, y_sample)
```

```python
import functools
import math

import jax
import jax.numpy as jnp
from jax import lax
from jax.experimental import pallas as pl
from jax.experimental.pallas import tpu as pltpu

F32 = jnp.float32
BF16 = jnp.bfloat16

D_MODEL = 1024
DEPTH = 2
RET_HEADS = 8
RET_QK_DIM = 64
RET_V_DIM = 128
RET_QK = RET_HEADS * RET_QK_DIM
RET_V = RET_HEADS * RET_V_DIM
ROPE_BASE = 10000.0
SSM_WIDTH = 1024
SSM_GROUP = 16
SSM_GROUPS = SSM_WIDTH // SSM_GROUP
SSM_STATE = 64
EIG_CLIP = -1e-4
D_FF = 2816
ALPHA = (2 * DEPTH) ** 0.25
LN_EPS = 1e-5

LANES = 128
MXU_DIM = 256
VMEM_LIMIT_BYTES = 56 * 1024 * 1024

TOKEN_TILE = 512
FF_CHUNKS = (1024, 1024, 768)
assert sum(FF_CHUNKS) == D_FF and all(w % MXU_DIM == 0 for w in FF_CHUNKS)
MERGE_COLS = 512
RET_CHUNK = MXU_DIM
S5_CHUNK = MXU_DIM // SSM_GROUP
S5_BATCH_TILE = 16
S5_GROUPS_PER_STEP = 2


def _resident(shape):
    nd = len(shape)
    return pl.BlockSpec(shape, lambda *_: (0,) * nd, pipeline_mode=pl.Buffered(1))


def _params(*sem):
    return pltpu.CompilerParams(dimension_semantics=sem, vmem_limit_bytes=VMEM_LIMIT_BYTES)


def _dot(a, b):
    return jnp.dot(a, b, preferred_element_type=F32)


def _sigmoid(x):
    return 1.0 / (1.0 + jnp.exp(-x))


def _layer_norm(y, g, b):
    mu = jnp.mean(y, axis=-1, keepdims=True)
    yc = y - mu
    var = jnp.mean(yc * yc, axis=-1, keepdims=True)
    return yc * lax.rsqrt(var + LN_EPS) * g + b


def _delayed_layer_norm(pre_ref, g_ref, b_ref, o_ref):
    @pl.when(pl.program_id(0) == 0)
    def _():
        pre_ref[...] = jnp.zeros_like(pre_ref)

    o_ref[...] = _layer_norm(pre_ref[...], g_ref[...], b_ref[...])


def _delayed_specs(rows, in_t0, out_t0, width):
    n = rows // TOKEN_TILE
    in_map = lambda i: (jnp.minimum(i, n - 1) + in_t0, 0)
    out_map = lambda i: (jnp.maximum(i - 1, 0) + out_t0, 0)
    return (n + 1,), in_map, pl.BlockSpec((TOKEN_TILE, width), out_map)


def _swiglu(x, wgu_ref, wdn_ref):
    xb = x.astype(BF16)
    acc = jnp.zeros(x.shape, F32)
    lo = 0
    for width in FF_CHUNKS:
        a = _dot(xb, wgu_ref[:, lo:lo + width])
        u = _dot(xb, wgu_ref[:, D_FF + lo:D_FF + lo + width])
        h = (a * _sigmoid(a) * u).astype(BF16)
        acc = acc + _dot(h, wdn_ref[lo:lo + width, :])
        lo += width
    return acc


def _ffn_ln_kernel(x_ref, wgu_ref, wdn_ref, g_ref, b_ref, *rest):
    o_ref, pre_ref = rest[-2:]
    _delayed_layer_norm(pre_ref, g_ref, b_ref, o_ref)
    x = x_ref[...]
    pre_ref[...] = ALPHA * x + 0.5 * _swiglu(x, wgu_ref, wdn_ref)


def _ffn_ln(x, w_gu, w_down, ln_g, ln_b, *, in_row0=0, rows=None, out_rows=None, out_row0=0, into=None):
    rows = x.shape[0] if rows is None else rows
    out_rows = rows if out_rows is None else out_rows
    grid, in_map, out_spec = _delayed_specs(rows, in_row0 // TOKEN_TILE, out_row0 // TOKEN_TILE, D_MODEL)
    args = [x, w_gu, w_down, ln_g.reshape(1, D_MODEL), ln_b.reshape(1, D_MODEL)]
    in_specs = [pl.BlockSpec((TOKEN_TILE, D_MODEL), in_map),
                _resident(w_gu.shape), _resident(w_down.shape),
                _resident((1, D_MODEL)), _resident((1, D_MODEL))]
    aliases = {}
    if into is not None:
        args.append(into)
        in_specs.append(pl.BlockSpec(memory_space=pl.ANY))
        aliases = {len(args) - 1: 0}
        out_rows = into.shape[0]
    return pl.pallas_call(
        _ffn_ln_kernel,
        grid=grid,
        in_specs=in_specs,
        out_specs=out_spec,
        out_shape=jax.ShapeDtypeStruct((out_rows, D_MODEL), F32),
        input_output_aliases=aliases,
        scratch_shapes=[pltpu.VMEM((TOKEN_TILE, D_MODEL), F32)],
        compiler_params=_params("arbitrary"),
        name="ffn_ln",
    )(*args)


def _in_proj_kernel(x_ref, w_ref, bg_ref, cos_ref, sin_ref,
                    q_ref, k_ref, v_ref, g_ref, u_ref, gr_ref, gs_ref):
    xb = x_ref[...].astype(BF16)

    def seg(lo, width):
        return _dot(xb, w_ref[:, lo:lo + width])

    lane = lax.broadcasted_iota(jnp.int32, (1, RET_QK), 1)
    first_half = (lane % RET_QK_DIM) < (RET_QK_DIM // 2)
    cos = cos_ref[...]
    sin = sin_ref[...]

    def rotary(t):
        half = RET_QK_DIM // 2
        swapped = jnp.where(first_half, pltpu.roll(t, RET_QK - half, axis=1), pltpu.roll(t, half, axis=1))
        return t * cos + swapped * sin

    base = 2 * RET_QK
    gate0 = base + 2 * RET_V + SSM_WIDTH
    gr_ref[...] = _sigmoid(seg(gate0, D_MODEL) + bg_ref[:, :D_MODEL]).astype(BF16)
    gs_ref[...] = _sigmoid(seg(gate0 + D_MODEL, D_MODEL) + bg_ref[:, D_MODEL:]).astype(BF16)
    q_ref[...] = rotary(seg(0, RET_QK)).astype(BF16)
    k_ref[...] = (rotary(seg(RET_QK, RET_QK)) * (RET_QK_DIM ** -0.5)).astype(BF16)
    v_ref[...] = seg(base, RET_V).astype(BF16)
    g_ref[...] = seg(base + RET_V, RET_V).astype(BF16)
    u_ref[...] = seg(base + 2 * RET_V, SSM_WIDTH).astype(BF16)


def _in_proj(x, w_in, b_gate, cos_t, sin_t, seq_len):
    t = x.shape[0]
    tiles_per_seq = seq_len // TOKEN_TILE
    tile = lambda w: pl.BlockSpec((TOKEN_TILE, w), lambda i: (i, 0))
    table = pl.BlockSpec((TOKEN_TILE, RET_QK), lambda i: (i % tiles_per_seq, 0))
    widths = (RET_QK, RET_QK, RET_V, RET_V, SSM_WIDTH, D_MODEL, D_MODEL)
    return pl.pallas_call(
        _in_proj_kernel,
        grid=(t // TOKEN_TILE,),
        in_specs=[tile(D_MODEL), _resident(w_in.shape), _resident((1, 2 * D_MODEL)), table, table],
        out_specs=[tile(w) for w in widths],
        out_shape=[jax.ShapeDtypeStruct((t, w), BF16) for w in widths],
        compiler_params=_params("parallel"),
        name="in_proj",
    )(x, w_in, b_gate.reshape(1, 2 * D_MODEL), cos_t, sin_t)


def _rotary_tables(seq_len):
    half = RET_QK_DIM // 2
    inv_freq = ROPE_BASE ** (-jnp.arange(half, dtype=F32) / half)
    ang = jnp.arange(seq_len, dtype=F32)[:, None] * inv_freq[None, :]
    cos = jnp.cos(ang)
    sin = jnp.sin(ang)
    cos_t = jnp.tile(jnp.concatenate([cos, cos], axis=-1), (1, RET_HEADS))
    sin_t = jnp.tile(jnp.concatenate([-sin, sin], axis=-1), (1, RET_HEADS))
    return cos_t, sin_t


def _retention_tables():
    c = RET_CHUNK
    log_gamma = jnp.log1p(-jnp.exp2(-5.0 - jnp.arange(RET_HEADS, dtype=F32)))
    idx = jnp.arange(c, dtype=F32)
    dist = jnp.abs(idx[:, None] - idx[None, :])
    decay = jnp.exp(log_gamma[:, None, None] * dist[None])
    expo = jnp.stack([idx + 1.0, c - idx, c - 1.0 - idx, idx], axis=0)
    edge = jnp.exp(log_gamma[:, None, None] * expo[None])
    edge = jnp.broadcast_to(edge[..., None], (RET_HEADS, 4, c, LANES))
    return decay, edge


def _retention_kernel(q_ref, k_ref, v_ref, g_ref, decay_ref, edge_ref, o_ref, kvf_ref, kvb_ref):
    c = RET_CHUNK
    nc = q_ref.shape[1] // c
    lane = lax.broadcasted_iota(jnp.int32, (1, LANES), 1)
    contract_rows = (((0,), (0,)), ((), ()))
    contract_lanes = (((1,), (1,)), ((), ()))
    for e in range(2):
        mine = (lane < RET_QK_DIM) if e == 0 else (lane >= RET_QK_DIM)
        vsl = slice(e * RET_V_DIM, (e + 1) * RET_V_DIM)
        xi_f, xi_b = edge_ref[e, 0], edge_ref[e, 1]
        zeta_f, zeta_b = edge_ref[e, 2], edge_ref[e, 3]
        chunk_decay = xi_f[c - 1:c, :]

        def k_masked(n):
            kc = k_ref[0, n * c:(n + 1) * c, :]
            return jnp.where(mine, kc, jnp.zeros_like(kc))

        state = jnp.zeros((LANES, RET_V_DIM), F32)
        for n in range(nc):
            kvf_ref[n] = state
            if n + 1 < nc:
                kz = (k_masked(n).astype(F32) * zeta_f).astype(BF16)
                kv = lax.dot_general(kz, v_ref[0, n * c:(n + 1) * c, vsl], contract_rows,
                                     preferred_element_type=F32)
                state = chunk_decay * state + kv
        state = jnp.zeros((LANES, RET_V_DIM), F32)
        for n in range(nc - 1, -1, -1):
            kvb_ref[n] = state
            if n > 0:
                kz = (k_masked(n).astype(F32) * zeta_b).astype(BF16)
                kv = lax.dot_general(kz, v_ref[0, n * c:(n + 1) * c, vsl], contract_rows,
                                     preferred_element_type=F32)
                state = chunk_decay * state + kv

        for n in range(nc):
            rows = slice(n * c, (n + 1) * c)
            qc = q_ref[0, rows, :]
            vc = v_ref[0, rows, vsl]
            scores = lax.dot_general(qc, k_masked(n), contract_lanes, preferred_element_type=F32)
            scores = scores * decay_ref[e]
            o = _dot(scores.astype(BF16), vc)
            qf = qc.astype(F32)
            o = o + _dot((qf * xi_f).astype(BF16), kvf_ref[n].astype(BF16))
            o = o + _dot((qf * xi_b).astype(BF16), kvb_ref[n].astype(BF16))
            mu = jnp.mean(o, axis=-1, keepdims=True)
            oc = o - mu
            var = jnp.mean(oc * oc, axis=-1, keepdims=True)
            on = oc * lax.rsqrt(var + LN_EPS)
            gate = g_ref[0, rows, vsl].astype(F32)
            o_ref[0, rows, vsl] = (gate * _sigmoid(gate) * on).astype(BF16)


def _retention(q, k, v, g, decay, edge):
    b, seq_len, _ = q.shape
    nc = seq_len // RET_CHUNK
    qk_spec = pl.BlockSpec((1, seq_len, LANES), lambda i, h: (i, 0, h))
    v_spec = pl.BlockSpec((1, seq_len, 2 * RET_V_DIM), lambda i, h: (i, 0, h))
    return pl.pallas_call(
        _retention_kernel,
        grid=(b, RET_HEADS // 2),
        in_specs=[qk_spec, qk_spec, v_spec, v_spec,
                  pl.BlockSpec((2, RET_CHUNK, RET_CHUNK), lambda i, h: (h, 0, 0)),
                  pl.BlockSpec((2, 4, RET_CHUNK, LANES), lambda i, h: (h, 0, 0, 0))],
        out_specs=v_spec,
        out_shape=jax.ShapeDtypeStruct((b, seq_len, RET_V), BF16),
        scratch_shapes=[pltpu.VMEM((nc, LANES, RET_V_DIM), F32), pltpu.VMEM((nc, LANES, RET_V_DIM), F32)],
        compiler_params=_params("parallel", "parallel"),
        name="retention",
    )(q, k, v, g, decay, edge)


def _s5_matrices(lam_re, lam_im, log_dt, b_re, b_im, c_re, c_im, d_skip):
    hi = lax.Precision.HIGHEST
    cs, n, p = S5_CHUNK, SSM_STATE, SSM_GROUP
    lr = jnp.minimum(lam_re.astype(F32), EIG_CLIP)
    li = lam_im.astype(F32)
    dt = jnp.exp(log_dt.astype(F32))[..., None]
    groups, width = SSM_GROUPS, cs * p
    steps = jnp.arange(cs + 1, dtype=F32)[None, None, :, None]
    mag = jnp.exp((lr * dt)[:, :, None, :] * steps)
    ang = (li * dt)[:, :, None, :] * steps
    pw_re = mag * jnp.cos(ang)
    pw_im = mag * jnp.sin(ang)
    ab_re, ab_im = pw_re[:, :, 1], pw_im[:, :, 1]
    den = lr * lr + li * li
    num_re = ab_re - 1.0
    f_re = ((num_re * lr + ab_im * li) / den)[:, :, None, :]
    f_im = ((ab_im * lr - num_re * li) / den)[:, :, None, :]
    br_t = b_re.astype(F32).transpose(0, 1, 3, 2)
    bi_t = b_im.astype(F32).transpose(0, 1, 3, 2)
    bb_re = f_re * br_t - f_im * bi_t
    bb_im = f_re * bi_t + f_im * br_t
    cr_t = c_re.astype(F32).transpose(0, 1, 3, 2)[:, :, :, None, :]
    ci_t = c_im.astype(F32).transpose(0, 1, 3, 2)[:, :, :, None, :]
    pwn_re = pw_re.transpose(0, 1, 3, 2)[..., None]
    pwn_im = pw_im.transpose(0, 1, 3, 2)[..., None]
    cp_re = cr_t * pwn_re - ci_t * pwn_im
    cp_im = cr_t * pwn_im + ci_t * pwn_re

    lhs = jnp.concatenate([bb_re, -bb_im], axis=-1)
    rhs = jnp.concatenate([cp_re[:, :, :, :cs], cp_im[:, :, :, :cs]], axis=2).reshape(2, groups, 2 * n, width)
    taps = jnp.einsum('dgin,dgnl->dgil', lhs, rhs, precision=hi)
    zeros = jnp.zeros((groups, p, width), F32)
    pad_f = jnp.concatenate([zeros, taps[0]], axis=-1)
    pad_b = jnp.concatenate([taps[1].reshape(groups, p, cs, p)[:, :, ::-1].reshape(groups, p, width), zeros], axis=-1)
    toep = jnp.stack([pad_f[:, :, width - p * s:2 * width - p * s]
                      + pad_b[:, :, p * (cs - 1 - s):p * (cs - 1 - s) + width] for s in range(cs)], axis=1)
    toep = toep.reshape(groups, width, width)

    def state_cols(pw_r, pw_i, bbr, bbi):
        pw_r, pw_i, bbr, bbi = pw_r[:, :, None, :], pw_i[:, :, None, :], bbr[:, None], bbi[:, None]
        shape = (groups, width, n)
        return (pw_r * bbr - pw_i * bbi).reshape(shape), (pw_r * bbi + pw_i * bbr).reshape(shape)

    sf_re, sf_im = state_cols(pw_re[0, :, :cs][:, ::-1], pw_im[0, :, :cs][:, ::-1], bb_re[0], bb_im[0])
    sb_re, sb_im = state_cols(pw_re[1, :, :cs], pw_im[1, :, :cs], bb_re[1], bb_im[1])
    w_state = jnp.concatenate([sf_re, sb_re, sf_im, sb_im], axis=-1)

    rows = lambda a: a.reshape(groups, n, width)
    w_out = jnp.concatenate([rows(cp_re[0, :, :, 1:]), rows(cp_re[1, :, :, 1:][:, :, ::-1]),
                             rows(-cp_im[0, :, :, 1:]), rows(-cp_im[1, :, :, 1:][:, :, ::-1])], axis=1)

    a_re = jnp.concatenate([pw_re[0, :, cs], pw_re[1, :, cs]], axis=-1)
    a_im = jnp.concatenate([pw_im[0, :, cs], pw_im[1, :, cs]], axis=-1)
    a_pow = jnp.stack([a_re, a_im], axis=1)
    d_lanes = jnp.tile(d_skip.astype(F32).reshape(SSM_GROUPS, 1, p), (1, 1, cs))
    return toep.astype(BF16), w_state.astype(BF16), w_out.astype(BF16), a_pow, d_lanes


def _s5_kernel(u_ref, toep_ref, wst_ref, wout_ref, apow_ref, d_ref, z_ref,
               loc_ref, fwd_re_ref, fwd_im_ref, bwd_re_ref, bwd_im_ref):
    ng, nch, tb, width = u_ref.shape
    for gi in range(ng):
        loc_ref[gi] = _dot(u_ref[gi].reshape(nch * tb, width), wst_ref[gi]).reshape(nch, tb, width)

    half = width // 2
    lane = lax.broadcasted_iota(jnp.int32, (1, half), 1)
    is_fwd = lane < SSM_STATE

    def step(i, carry):
        j = nch - 1 - i
        out = []
        for gi in range(ng):
            h_re, h_im = carry[gi]
            a_re = apow_ref[gi, 0:1, :]
            a_im = apow_ref[gi, 1:2, :]
            fwd_re_ref[gi, i] = h_re
            fwd_im_ref[gi, i] = h_im
            bwd_re_ref[gi, j] = h_re
            bwd_im_ref[gi, j] = h_im
            s_re = jnp.where(is_fwd, loc_ref[gi, i, :, :half], loc_ref[gi, j, :, :half])
            s_im = jnp.where(is_fwd, loc_ref[gi, i, :, half:], loc_ref[gi, j, :, half:])
            out.append((a_re * h_re - a_im * h_im + s_re, a_re * h_im + a_im * h_re + s_im))
        return tuple(out)

    zero = jnp.zeros((tb, half), F32)
    lax.fori_loop(0, nch, step, tuple((zero, zero) for _ in range(ng)))

    for gi in range(ng):
        u = u_ref[gi].reshape(nch * tb, width)
        st_re = jnp.where(is_fwd, fwd_re_ref[gi], bwd_re_ref[gi]).reshape(nch * tb, half)
        st_im = jnp.where(is_fwd, fwd_im_ref[gi], bwd_im_ref[gi]).reshape(nch * tb, half)
        state = jnp.concatenate([st_re, st_im], axis=-1).astype(BF16)
        y = _dot(u, toep_ref[gi]) + _dot(state, wout_ref[gi]) + u.astype(F32) * d_ref[gi]
        gelu = 0.5 * y * (1.0 + jnp.tanh(math.sqrt(2.0 / math.pi) * (y + 0.044715 * (y * y * y))))
        z_ref[gi] = gelu.astype(BF16).reshape(nch, tb, width)


def _s5(u_g, toep, w_state, w_out, a_pow, d_lanes):
    groups, nch, b, width = u_g.shape
    tb = S5_BATCH_TILE if b % S5_BATCH_TILE == 0 else b
    ng = S5_GROUPS_PER_STEP
    io_spec = pl.BlockSpec((ng, nch, tb, width), lambda g, i: (g, 0, i, 0))
    mat_spec = pl.BlockSpec((ng, width, width), lambda g, i: (g, 0, 0))
    half = width // 2
    return pl.pallas_call(
        _s5_kernel,
        grid=(groups // ng, b // tb),
        in_specs=[io_spec, mat_spec, mat_spec, mat_spec,
                  pl.BlockSpec((ng, 2, half), lambda g, i: (g, 0, 0)),
                  pl.BlockSpec((ng, 1, width), lambda g, i: (g, 0, 0))],
        out_specs=io_spec,
        out_shape=jax.ShapeDtypeStruct(u_g.shape, BF16),
        scratch_shapes=[pltpu.VMEM((ng, nch, tb, width), F32)] + [pltpu.VMEM((ng, nch, tb, half), F32)] * 4,
        compiler_params=_params("parallel", "parallel"),
        name="s5",
    )(u_g, toep, w_state, w_out, a_pow, d_lanes)


GROUPS_PER_BLOCK = LANES // SSM_GROUP
RUN_STEPS = LANES // SSM_GROUP
RUNS = S5_CHUNK // RUN_STEPS
REGROUP_WIDTH = RUN_STEPS * LANES
REGROUP_TOKENS = 512
STAGE_PAD = 8


def _selection_matrix():
    row = jnp.arange(REGROUP_WIDTH)
    step, rem = row // LANES, row % LANES
    col = (rem // SSM_GROUP) * LANES + step * SSM_GROUP + rem % SSM_GROUP
    return (col[:, None] == jnp.arange(REGROUP_WIDTH)[None, :]).astype(BF16)


def _to_groups_kernel(u_ref, sel_ref, o_ref, stage_ref, xcat_ref):
    tb, lc, _ = u_ref.shape
    nr = lc // S5_CHUNK
    slab = lc + STAGE_PAD
    for b in range(tb):
        stage_ref[b * slab:b * slab + lc, :] = u_ref[b].astype(F32)
    for r in range(nr):
        for s in range(S5_CHUNK):
            piece = stage_ref[pl.ds(r * S5_CHUNK + s, tb, stride=slab), :]
            run, step = divmod(s, RUN_STEPS)
            xcat_ref[run, r * tb:(r + 1) * tb, step * LANES:(step + 1) * LANES] = piece.astype(BF16)
    for run in range(RUNS):
        grouped = _dot(xcat_ref[run], sel_ref[...]).astype(BF16)
        for gam in range(GROUPS_PER_BLOCK):
            o_ref[gam, :, :, run * LANES:(run + 1) * LANES] = (
                grouped[:, gam * LANES:(gam + 1) * LANES].reshape(nr, tb, LANES))


def _to_tokens_kernel(z_ref, selt_ref, o_ref, stage_ref):
    _, nr, tb, _ = z_ref.shape
    lc = nr * S5_CHUNK
    slab = lc + STAGE_PAD
    for run in range(RUNS):
        zcat = jnp.concatenate([z_ref[gam, :, :, run * LANES:(run + 1) * LANES].reshape(nr * tb, LANES)
                                for gam in range(GROUPS_PER_BLOCK)], axis=-1)
        zt = _dot(zcat, selt_ref[...])
        for r in range(nr):
            for step in range(RUN_STEPS):
                t = run * RUN_STEPS + step
                stage_ref[pl.ds(r * S5_CHUNK + t, tb, stride=slab), :] = (
                    zt[r * tb:(r + 1) * tb, step * LANES:(step + 1) * LANES])
    for b in range(tb):
        o_ref[b] = stage_ref[b * slab:b * slab + lc, :].astype(BF16)


def _regroup_specs(b, seq_len):
    tb = S5_BATCH_TILE if b % S5_BATCH_TILE == 0 else b
    lc = REGROUP_TOKENS
    grid = (b // tb, seq_len // lc, SSM_WIDTH // LANES)
    token_spec = pl.BlockSpec((tb, lc, LANES), lambda i, c, j: (i, c, j))
    group_spec = pl.BlockSpec((GROUPS_PER_BLOCK, lc // S5_CHUNK, tb, S5_CHUNK * SSM_GROUP),
                              lambda i, c, j: (j, c, i, 0))
    stage = pltpu.VMEM((tb * (lc + STAGE_PAD), LANES), F32)
    return tb, grid, token_spec, group_spec, stage


def _to_group_major(u, sel, b, seq_len):
    tb, grid, token_spec, group_spec, stage = _regroup_specs(b, seq_len)
    return pl.pallas_call(
        _to_groups_kernel,
        grid=grid,
        in_specs=[token_spec, _resident(sel.shape)],
        out_specs=group_spec,
        out_shape=jax.ShapeDtypeStruct((SSM_GROUPS, seq_len // S5_CHUNK, b, S5_CHUNK * SSM_GROUP), BF16),
        scratch_shapes=[stage, pltpu.VMEM((RUNS, REGROUP_TOKENS // S5_CHUNK * tb, REGROUP_WIDTH), BF16)],
        compiler_params=_params("parallel", "parallel", "parallel"),
        name="to_groups",
    )(u, sel)


def _to_token_major(z_g, sel_t, b, seq_len):
    tb, grid, token_spec, group_spec, stage = _regroup_specs(b, seq_len)
    return pl.pallas_call(
        _to_tokens_kernel,
        grid=grid,
        in_specs=[group_spec, _resident(sel_t.shape)],
        out_specs=token_spec,
        out_shape=jax.ShapeDtypeStruct((b, seq_len, SSM_WIDTH), BF16),
        scratch_shapes=[stage],
        compiler_params=_params("parallel", "parallel", "parallel"),
        name="to_tokens",
    )(z_g, sel_t)


def _merge_ffn_kernel(x_ref, oret_ref, z_ref, gr_ref, gs_ref, wo_ref, wglu_ref, wout_ref, g2_ref, b2_ref,
                      wgu_ref, wdn_ref, g3_ref, b3_ref, o_ref, x2_ref):
    @pl.when(pl.program_id(0) == 0)
    def _():
        x2_ref[...] = jnp.zeros_like(x2_ref)

    x2 = x2_ref[...]
    o_ref[...] = _layer_norm(ALPHA * x2 + 0.5 * _swiglu(x2, wgu_ref, wdn_ref), g3_ref[...], b3_ref[...])

    oret = oret_ref[...]
    z = z_ref[...]
    pieces = []
    for lo in range(0, D_MODEL, MERGE_COLS):
        cols = slice(lo, lo + MERGE_COLS)
        y_ret = _dot(oret, wo_ref[:, cols])
        val = _dot(z, wglu_ref[:, cols])
        gate = _dot(z, wglu_ref[:, D_MODEL + lo:D_MODEL + lo + MERGE_COLS])
        y_ssm = val * _sigmoid(gate)
        pieces.append((gr_ref[:, cols].astype(F32) * y_ret + gs_ref[:, cols].astype(F32) * y_ssm).astype(BF16))
    mix = _dot(jnp.concatenate(pieces, axis=-1), wout_ref[...])
    x2_ref[...] = _layer_norm(ALPHA * x_ref[...] + mix, g2_ref[...], b2_ref[...])


def _merge_ffn(x, oret, z, gr, gs, w_o, w_glu, w_out, ln2_g, ln2_b, w_gu, w_down, ln3_g, ln3_b, *,
               in_row0=0, rows=None):
    rows = x.shape[0] if rows is None else rows
    grid, in_map, out_spec = _delayed_specs(rows, in_row0 // TOKEN_TILE, 0, D_MODEL)
    tile = pl.BlockSpec((TOKEN_TILE, D_MODEL), in_map)
    vec = lambda v: v.reshape(1, D_MODEL)
    return pl.pallas_call(
        _merge_ffn_kernel,
        grid=grid,
        in_specs=[tile] * 5 + [_resident(w_o.shape), _resident(w_glu.shape), _resident(w_out.shape),
                               _resident((1, D_MODEL)), _resident((1, D_MODEL)),
                               _resident(w_gu.shape), _resident(w_down.shape),
                               _resident((1, D_MODEL)), _resident((1, D_MODEL))],
        out_specs=out_spec,
        out_shape=jax.ShapeDtypeStruct((rows, D_MODEL), F32),
        scratch_shapes=[pltpu.VMEM((TOKEN_TILE, D_MODEL), F32)],
        compiler_params=_params("arbitrary"),
        name="merge_ffn",
    )(x, oret, z, gr, gs, w_o, w_glu, w_out, vec(ln2_g), vec(ln2_b), w_gu, w_down, vec(ln3_g), vec(ln3_b))


def _trunk(xs, params):
    (ffn1_w_gu, ffn1_w_down, ln1_g, ln1_b, w_in, b_gate, ret_w_o, s5_lam_re, s5_lam_im, s5_log_dt,
     s5_b_re, s5_b_im, s5_c_re, s5_c_im, s5_d, s5_w_glu, w_out, ln2_g, ln2_b,
     ffn2_w_gu, ffn2_w_down, ln3_g, ln3_b) = params
    seq_len = xs[0].shape[1]
    assert all(x.shape[1:] == (seq_len, D_MODEL) for x in xs)
    assert seq_len % TOKEN_TILE == 0 and seq_len % RET_CHUNK == 0 and seq_len % REGROUP_TOKENS == 0
    sizes = [x.shape[0] * seq_len for x in xs]
    starts = [sum(sizes[:i]) for i in range(len(xs))]
    b, t = sum(x.shape[0] for x in xs), sum(sizes)
    cos_t, sin_t = _rotary_tables(seq_len)
    decay, edge = _retention_tables()
    sel = _selection_matrix()
    bf = lambda w: w.astype(BF16)
    x = None
    for l in range(DEPTH):
        ffn1 = (bf(ffn1_w_gu[l]), bf(ffn1_w_down[l]), ln1_g[l], ln1_b[l])
        if l == 0:
            for xi, size, start in zip(xs, sizes, starts):
                x = _ffn_ln(xi.reshape(size, D_MODEL), *ffn1, out_rows=t, out_row0=start, into=x)
        else:
            x = _ffn_ln(x, *ffn1)
        q, k, v, g, u, gr, gs = _in_proj(x, bf(w_in[l]), b_gate[l], cos_t, sin_t, seq_len)
        shape3 = lambda a: a.reshape(b, seq_len, a.shape[-1])
        oret = _retention(shape3(q), shape3(k), shape3(v), shape3(g), decay, edge)
        mats = _s5_matrices(s5_lam_re[l], s5_lam_im[l], s5_log_dt[l], s5_b_re[l], s5_b_im[l],
                            s5_c_re[l], s5_c_im[l], s5_d[l])
        z = _to_token_major(_s5(_to_group_major(shape3(u), sel, b, seq_len), *mats), sel.T, b, seq_len)
        tail = (x, oret.reshape(t, RET_V), z.reshape(t, SSM_WIDTH), gr, gs,
                bf(ret_w_o[l]), bf(s5_w_glu[l]), bf(w_out[l]), ln2_g[l], ln2_b[l],
                bf(ffn2_w_gu[l]), bf(ffn2_w_down[l]), ln3_g[l], ln3_b[l])
        if l < DEPTH - 1:
            x = _merge_ffn(*tail)
        else:
            ys = [_merge_ffn(*tail, in_row0=start, rows=size) for size, start in zip(sizes, starts)]
    return [y.reshape(xi.shape) for y, xi in zip(ys, xs)]


def kernel(x_prompt, x_sample, ffn1_w_gu, ffn1_w_down, ln1_g, ln1_b, w_in, b_gate, ret_w_o, s5_lam_re, s5_lam_im, s5_log_dt, s5_b_re, s5_b_im, s5_c_re, s5_c_im, s5_d, s5_w_glu, w_out, ln2_g, ln2_b, ffn2_w_gu, ffn2_w_down, ln3_g, ln3_b):
    params = (ffn1_w_gu, ffn1_w_down, ln1_g, ln1_b, w_in, b_gate, ret_w_o, s5_lam_re, s5_lam_im, s5_log_dt,
              s5_b_re, s5_b_im, s5_c_re, s5_c_im, s5_d, s5_w_glu, w_out, ln2_g, ln2_b,
              ffn2_w_gu, ffn2_w_down, ln3_g, ln3_b)
    y_prompt, y_sample = _trunk([x_prompt, x_sample], params)
    return (y_prompt, y_sample)
```

```python
import functools
import math

import jax
import jax.numpy as jnp
from jax import lax
from jax.experimental import pallas as pl
from jax.experimental.pallas import tpu as pltpu

F32 = jnp.float32
BF16 = jnp.bfloat16

D_MODEL = 1024
DEPTH = 2
RET_HEADS = 8
RET_QK_DIM = 64
RET_V_DIM = 128
RET_QK = RET_HEADS * RET_QK_DIM
RET_V = RET_HEADS * RET_V_DIM
ROPE_BASE = 10000.0
SSM_WIDTH = 1024
SSM_GROUP = 16
SSM_GROUPS = SSM_WIDTH // SSM_GROUP
SSM_STATE = 64
EIG_CLIP = -1e-4
D_FF = 2816
ALPHA = (2 * DEPTH) ** 0.25
LN_EPS = 1e-5

LANES = 128
MXU_DIM = 256
VMEM_LIMIT_BYTES = 56 * 1024 * 1024

TOKEN_TILE = 512
FF_CHUNKS = (1024, 1024, 768)
assert sum(FF_CHUNKS) == D_FF and all(w % MXU_DIM == 0 for w in FF_CHUNKS)
MERGE_COLS = 512
RET_CHUNK = MXU_DIM
S5_CHUNK = MXU_DIM // SSM_GROUP
S5_BATCH_TILE = 16
S5_GROUPS_PER_STEP = 2


def _resident(shape):
    nd = len(shape)
    return pl.BlockSpec(shape, lambda *_: (0,) * nd, pipeline_mode=pl.Buffered(1))


def _layer(stacked, layer):
    nd = stacked.ndim - 1
    return pl.BlockSpec((None,) + stacked.shape[1:], lambda *_: (layer,) + (0,) * nd,
                        pipeline_mode=pl.Buffered(1))


def _params(*sem):
    return pltpu.CompilerParams(dimension_semantics=sem, vmem_limit_bytes=VMEM_LIMIT_BYTES)


def _dot(a, b):
    return jnp.dot(a, b, preferred_element_type=F32)


def _sigmoid(x):
    return 1.0 / (1.0 + jnp.exp(-x))


def _layer_norm(y, g, b):
    mu = jnp.mean(y, axis=-1, keepdims=True)
    yc = y - mu
    var = jnp.mean(yc * yc, axis=-1, keepdims=True)
    return yc * lax.rsqrt(var + LN_EPS) * g + b


def _delayed_layer_norm(pre_ref, g_ref, b_ref, o_ref):
    @pl.when(pl.program_id(0) == 0)
    def _():
        pre_ref[...] = jnp.zeros_like(pre_ref)

    o_ref[...] = _layer_norm(pre_ref[...], g_ref[...], b_ref[...])


def _delayed_specs(rows, in_t0, out_t0, width):
    n = rows // TOKEN_TILE
    in_map = lambda i: (jnp.minimum(i, n - 1) + in_t0, 0)
    out_map = lambda i: (jnp.maximum(i - 1, 0) + out_t0, 0)
    return (n + 1,), in_map, pl.BlockSpec((TOKEN_TILE, width), out_map)


def _swiglu(x, wgu_ref, wdn_ref):
    xb = x.astype(BF16)
    acc = jnp.zeros(x.shape, F32)
    lo = 0
    for width in FF_CHUNKS:
        a = _dot(xb, wgu_ref[:, lo:lo + width])
        u = _dot(xb, wgu_ref[:, D_FF + lo:D_FF + lo + width])
        h = (a * _sigmoid(a) * u).astype(BF16)
        acc = acc + _dot(h, wdn_ref[lo:lo + width, :])
        lo += width
    return acc


def _ffn_ln_kernel(x_ref, wgu_ref, wdn_ref, g_ref, b_ref, *rest):
    o_ref, pre_ref = rest[-2:]
    _delayed_layer_norm(pre_ref, g_ref, b_ref, o_ref)
    x = x_ref[...]
    pre_ref[...] = ALPHA * x + 0.5 * _swiglu(x, wgu_ref, wdn_ref)


def _ffn_ln(x, layer, w_gu, w_down, ln_g, ln_b, *, in_row0=0, rows=None, out_rows=None, out_row0=0, into=None):
    rows = x.shape[0] if rows is None else rows
    out_rows = rows if out_rows is None else out_rows
    grid, in_map, out_spec = _delayed_specs(rows, in_row0 // TOKEN_TILE, out_row0 // TOKEN_TILE, D_MODEL)
    args = [x, w_gu, w_down, ln_g, ln_b]
    in_specs = [pl.BlockSpec((TOKEN_TILE, D_MODEL), in_map)] + [_layer(a, layer) for a in args[1:]]
    aliases = {}
    if into is not None:
        args.append(into)
        in_specs.append(pl.BlockSpec(memory_space=pl.ANY))
        aliases = {len(args) - 1: 0}
        out_rows = into.shape[0]
    return pl.pallas_call(
        _ffn_ln_kernel,
        grid=grid,
        in_specs=in_specs,
        out_specs=out_spec,
        out_shape=jax.ShapeDtypeStruct((out_rows, D_MODEL), F32),
        input_output_aliases=aliases,
        scratch_shapes=[pltpu.VMEM((TOKEN_TILE, D_MODEL), F32)],
        compiler_params=_params("arbitrary"),
        name="ffn_ln",
    )(*args)


def _in_proj_kernel(x_ref, w_ref, bg_ref, cos_ref, sin_ref,
                    q_ref, k_ref, v_ref, g_ref, u_ref, gr_ref, gs_ref):
    xb = x_ref[...].astype(BF16)

    def seg(lo, width):
        return _dot(xb, w_ref[:, lo:lo + width])

    lane = lax.broadcasted_iota(jnp.int32, (1, RET_QK), 1)
    first_half = (lane % RET_QK_DIM) < (RET_QK_DIM // 2)
    cos = cos_ref[...]
    sin = sin_ref[...]

    def rotary(t):
        half = RET_QK_DIM // 2
        swapped = jnp.where(first_half, pltpu.roll(t, RET_QK - half, axis=1), pltpu.roll(t, half, axis=1))
        return t * cos + swapped * sin

    base = 2 * RET_QK
    gate0 = base + 2 * RET_V + SSM_WIDTH
    gr_ref[...] = _sigmoid(seg(gate0, D_MODEL) + bg_ref[:, :D_MODEL]).astype(BF16)
    gs_ref[...] = _sigmoid(seg(gate0 + D_MODEL, D_MODEL) + bg_ref[:, D_MODEL:]).astype(BF16)
    g = seg(base + RET_V, RET_V)
    g_ref[...] = (g * _sigmoid(g)).astype(BF16)
    q_ref[...] = rotary(seg(0, RET_QK)).astype(BF16)
    k_ref[...] = (rotary(seg(RET_QK, RET_QK)) * (RET_QK_DIM ** -0.5)).astype(BF16)
    v_ref[...] = seg(base, RET_V).astype(BF16)
    u_ref[...] = seg(base + 2 * RET_V, SSM_WIDTH).astype(BF16)


def _in_proj(x, layer, w_in, b_gate, cos_t, sin_t, seq_len):
    t = x.shape[0]
    tiles_per_seq = seq_len // TOKEN_TILE
    tile = lambda w: pl.BlockSpec((TOKEN_TILE, w), lambda i: (i, 0))
    table = pl.BlockSpec((TOKEN_TILE, RET_QK), lambda i: (i % tiles_per_seq, 0))
    widths = (RET_QK, RET_QK, RET_V, RET_V, SSM_WIDTH, D_MODEL, D_MODEL)
    return pl.pallas_call(
        _in_proj_kernel,
        grid=(t // TOKEN_TILE,),
        in_specs=[tile(D_MODEL), _layer(w_in, layer), _layer(b_gate, layer), table, table],
        out_specs=[tile(w) for w in widths],
        out_shape=[jax.ShapeDtypeStruct((t, w), BF16) for w in widths],
        compiler_params=_params("parallel"),
        name="in_proj",
    )(x, w_in, b_gate, cos_t, sin_t)


def _rotary_tables(seq_len):
    half = RET_QK_DIM // 2
    inv_freq = ROPE_BASE ** (-jnp.arange(half, dtype=F32) / half)
    ang = jnp.arange(seq_len, dtype=F32)[:, None] * inv_freq[None, :]
    cos = jnp.cos(ang)
    sin = jnp.sin(ang)
    cos_t = jnp.tile(jnp.concatenate([cos, cos], axis=-1), (1, RET_HEADS))
    sin_t = jnp.tile(jnp.concatenate([-sin, sin], axis=-1), (1, RET_HEADS))
    return cos_t, sin_t


def _retention_tables():
    c = RET_CHUNK
    log_gamma = jnp.log1p(-jnp.exp2(-5.0 - jnp.arange(RET_HEADS, dtype=F32)))
    idx = jnp.arange(c, dtype=F32)
    dist = jnp.abs(idx[:, None] - idx[None, :])
    decay = jnp.exp(log_gamma[:, None, None] * dist[None])
    expo = jnp.stack([idx + 1.0, c - idx, c - 1.0 - idx, idx], axis=0)
    edge = jnp.exp(log_gamma[:, None, None] * expo[None])
    edge = jnp.broadcast_to(edge[..., None], (RET_HEADS, 4, c, LANES))
    lane_head = jnp.arange(LANES) // RET_QK_DIM
    own = (lane_head[None, :] == (jnp.arange(RET_HEADS) % 2)[:, None]).astype(F32)
    is_zeta = jnp.array([0.0, 0.0, 1.0, 1.0], F32)[None, :, None, None]
    edge = edge * (1.0 - is_zeta + is_zeta * own[:, None, None, :])
    return decay, edge


def _retention_kernel(q_ref, k_ref, v_ref, sg_ref, decay_ref, edge_ref, o_ref, loc_ref, st_ref):
    c = RET_CHUNK
    nc = q_ref.shape[1] // c
    lane = lax.broadcasted_iota(jnp.int32, (1, LANES), 1)
    contract_rows = (((0,), (0,)), ((), ()))
    contract_lanes = (((1,), (1,)), ((), ()))
    for e in range(2):
        mine = (lane < RET_QK_DIM) if e == 0 else (lane >= RET_QK_DIM)
        vsl = slice(e * RET_V_DIM, (e + 1) * RET_V_DIM)
        xi_f, xi_b = edge_ref[e, 0], edge_ref[e, 1]
        zeta_f, zeta_b = edge_ref[e, 2], edge_ref[e, 3]
        chunk_decay = xi_f[c - 1:c, :]

        for n in range(nc):
            kf = k_ref[0, n * c:(n + 1) * c, :].astype(F32)
            kz = jnp.concatenate([(kf * zeta_f).astype(BF16), (kf * zeta_b).astype(BF16)], axis=-1)
            loc_ref[n] = lax.dot_general(kz, v_ref[0, n * c:(n + 1) * c, vsl], contract_rows,
                                         preferred_element_type=F32)
        state = jnp.zeros((LANES, RET_V_DIM), F32)
        for n in range(nc):
            st_ref[n, :LANES, :] = state
            state = chunk_decay * state + loc_ref[n, :LANES, :]
        state = jnp.zeros((LANES, RET_V_DIM), F32)
        for n in range(nc - 1, -1, -1):
            st_ref[n, LANES:, :] = state
            state = chunk_decay * state + loc_ref[n, LANES:, :]

        for n in range(nc):
            rows = slice(n * c, (n + 1) * c)
            qc = q_ref[0, rows, :]
            kc = k_ref[0, rows, :]
            vc = v_ref[0, rows, vsl]
            scores = lax.dot_general(qc, jnp.where(mine, kc, jnp.zeros_like(kc)), contract_lanes,
                                     preferred_element_type=F32)
            scores = scores * decay_ref[e]
            qf = qc.astype(F32)
            q_edges = jnp.concatenate([(qf * xi_f).astype(BF16), (qf * xi_b).astype(BF16)], axis=-1)
            o = _dot(scores.astype(BF16), vc) + _dot(q_edges, st_ref[n].astype(BF16))
            mu = jnp.mean(o, axis=-1, keepdims=True)
            oc = o - mu
            var = jnp.mean(oc * oc, axis=-1, keepdims=True)
            on = oc * lax.rsqrt(var + LN_EPS)
            o_ref[0, rows, vsl] = (sg_ref[0, rows, vsl].astype(F32) * on).astype(BF16)


def _retention(q, k, v, g, decay, edge):
    b, seq_len, _ = q.shape
    nc = seq_len // RET_CHUNK
    qk_spec = pl.BlockSpec((1, seq_len, LANES), lambda i, h: (i, 0, h))
    v_spec = pl.BlockSpec((1, seq_len, 2 * RET_V_DIM), lambda i, h: (i, 0, h))
    return pl.pallas_call(
        _retention_kernel,
        grid=(b, RET_HEADS // 2),
        in_specs=[qk_spec, qk_spec, v_spec, v_spec,
                  pl.BlockSpec((2, RET_CHUNK, RET_CHUNK), lambda i, h: (h, 0, 0)),
                  pl.BlockSpec((2, 4, RET_CHUNK, LANES), lambda i, h: (h, 0, 0, 0))],
        out_specs=v_spec,
        out_shape=jax.ShapeDtypeStruct((b, seq_len, RET_V), BF16),
        scratch_shapes=[pltpu.VMEM((nc, 2 * LANES, RET_V_DIM), F32), pltpu.VMEM((nc, 2 * LANES, RET_V_DIM), F32)],
        compiler_params=_params("parallel", "parallel"),
        name="retention",
    )(q, k, v, g, decay, edge)


def _s5_matrices(lam_re, lam_im, log_dt, b_re, b_im, c_re, c_im, d_skip):
    hi = lax.Precision.HIGHEST
    cs, n, p = S5_CHUNK, SSM_STATE, SSM_GROUP
    lr = jnp.minimum(lam_re.astype(F32), EIG_CLIP)
    li = lam_im.astype(F32)
    dt = jnp.exp(log_dt.astype(F32))[..., None]
    groups, width = SSM_GROUPS, cs * p
    steps = jnp.arange(cs + 1, dtype=F32)[None, None, :, None]
    mag = jnp.exp((lr * dt)[:, :, None, :] * steps)
    ang = (li * dt)[:, :, None, :] * steps
    pw_re = mag * jnp.cos(ang)
    pw_im = mag * jnp.sin(ang)
    ab_re, ab_im = pw_re[:, :, 1], pw_im[:, :, 1]
    den = lr * lr + li * li
    num_re = ab_re - 1.0
    f_re = ((num_re * lr + ab_im * li) / den)[:, :, None, :]
    f_im = ((ab_im * lr - num_re * li) / den)[:, :, None, :]
    br_t = b_re.astype(F32).transpose(0, 1, 3, 2)
    bi_t = b_im.astype(F32).transpose(0, 1, 3, 2)
    bb_re = f_re * br_t - f_im * bi_t
    bb_im = f_re * bi_t + f_im * br_t
    cr_t = c_re.astype(F32).transpose(0, 1, 3, 2)[:, :, :, None, :]
    ci_t = c_im.astype(F32).transpose(0, 1, 3, 2)[:, :, :, None, :]
    pwn_re = pw_re.transpose(0, 1, 3, 2)[..., None]
    pwn_im = pw_im.transpose(0, 1, 3, 2)[..., None]
    cp_re = cr_t * pwn_re - ci_t * pwn_im
    cp_im = cr_t * pwn_im + ci_t * pwn_re

    lhs = jnp.concatenate([bb_re, -bb_im], axis=-1)
    rhs = jnp.concatenate([cp_re[:, :, :, :cs], cp_im[:, :, :, :cs]], axis=2).reshape(2, groups, 2 * n, width)
    taps = jnp.einsum('dgin,dgnl->dgil', lhs, rhs, precision=hi)
    zeros = jnp.zeros((groups, p, width), F32)
    pad_f = jnp.concatenate([zeros, taps[0]], axis=-1)
    pad_b = jnp.concatenate([taps[1].reshape(groups, p, cs, p)[:, :, ::-1].reshape(groups, p, width), zeros], axis=-1)
    toep = jnp.stack([pad_f[:, :, width - p * s:2 * width - p * s]
                      + pad_b[:, :, p * (cs - 1 - s):p * (cs - 1 - s) + width] for s in range(cs)], axis=1)
    toep = toep.reshape(groups, width, width)

    def state_cols(pw_r, pw_i, bbr, bbi):
        pw_r, pw_i, bbr, bbi = pw_r[:, :, None, :], pw_i[:, :, None, :], bbr[:, None], bbi[:, None]
        shape = (groups, width, n)
        return (pw_r * bbr - pw_i * bbi).reshape(shape), (pw_r * bbi + pw_i * bbr).reshape(shape)

    sf_re, sf_im = state_cols(pw_re[0, :, :cs][:, ::-1], pw_im[0, :, :cs][:, ::-1], bb_re[0], bb_im[0])
    sb_re, sb_im = state_cols(pw_re[1, :, :cs], pw_im[1, :, :cs], bb_re[1], bb_im[1])
    w_state = jnp.concatenate([sf_re, sb_re, sf_im, sb_im], axis=-1)

    rows = lambda a: a.reshape(groups, n, width)
    w_out = jnp.concatenate([rows(cp_re[0, :, :, 1:]), rows(cp_re[1, :, :, 1:][:, :, ::-1]),
                             rows(-cp_im[0, :, :, 1:]), rows(-cp_im[1, :, :, 1:][:, :, ::-1])], axis=1)

    a_re = jnp.concatenate([pw_re[0, :, cs], pw_re[1, :, cs]], axis=-1)
    a_im = jnp.concatenate([pw_im[0, :, cs], pw_im[1, :, cs]], axis=-1)
    a_pow = jnp.stack([a_re, a_im], axis=1)
    d_lanes = jnp.tile(d_skip.astype(F32).reshape(SSM_GROUPS, 1, p), (1, 1, cs))
    return toep.astype(BF16), w_state.astype(BF16), w_out.astype(BF16), a_pow, d_lanes


def _s5_kernel(u_ref, toep_ref, wst_ref, wout_ref, apow_ref, d_ref, z_ref,
               loc_ref, fwd_re_ref, fwd_im_ref, bwd_re_ref, bwd_im_ref):
    ng, nch, tb, width = u_ref.shape
    for gi in range(ng):
        loc_ref[gi] = _dot(u_ref[gi].reshape(nch * tb, width), wst_ref[gi]).reshape(nch, tb, width)

    half = width // 2
    lane = lax.broadcasted_iota(jnp.int32, (1, half), 1)
    is_fwd = lane < SSM_STATE

    def step(i, carry):
        j = nch - 1 - i
        out = []
        for gi in range(ng):
            h_re, h_im = carry[gi]
            a_re = apow_ref[gi, 0:1, :]
            a_im = apow_ref[gi, 1:2, :]
            fwd_re_ref[gi, i] = h_re
            fwd_im_ref[gi, i] = h_im
            bwd_re_ref[gi, j] = h_re
            bwd_im_ref[gi, j] = h_im
            s_re = jnp.where(is_fwd, loc_ref[gi, i, :, :half], loc_ref[gi, j, :, :half])
            s_im = jnp.where(is_fwd, loc_ref[gi, i, :, half:], loc_ref[gi, j, :, half:])
            out.append((a_re * h_re - a_im * h_im + s_re, a_re * h_im + a_im * h_re + s_im))
        return tuple(out)

    zero = jnp.zeros((tb, half), F32)
    lax.fori_loop(0, nch, step, tuple((zero, zero) for _ in range(ng)))

    for gi in range(ng):
        u = u_ref[gi].reshape(nch * tb, width)
        st_re = jnp.where(is_fwd, fwd_re_ref[gi], bwd_re_ref[gi]).reshape(nch * tb, half)
        st_im = jnp.where(is_fwd, fwd_im_ref[gi], bwd_im_ref[gi]).reshape(nch * tb, half)
        state = jnp.concatenate([st_re, st_im], axis=-1).astype(BF16)
        y = _dot(u, toep_ref[gi]) + _dot(state, wout_ref[gi]) + u.astype(F32) * d_ref[gi]
        gelu = 0.5 * y * (1.0 + jnp.tanh(math.sqrt(2.0 / math.pi) * (y + 0.044715 * (y * y * y))))
        z_ref[gi] = gelu.astype(BF16).reshape(nch, tb, width)


def _s5(u_g, toep, w_state, w_out, a_pow, d_lanes):
    groups, nch, b, width = u_g.shape
    tb = S5_BATCH_TILE if b % S5_BATCH_TILE == 0 else b
    ng = S5_GROUPS_PER_STEP
    io_spec = pl.BlockSpec((ng, nch, tb, width), lambda g, i: (g, 0, i, 0))
    mat_spec = pl.BlockSpec((ng, width, width), lambda g, i: (g, 0, 0))
    half = width // 2
    return pl.pallas_call(
        _s5_kernel,
        grid=(groups // ng, b // tb),
        in_specs=[io_spec, mat_spec, mat_spec, mat_spec,
                  pl.BlockSpec((ng, 2, half), lambda g, i: (g, 0, 0)),
                  pl.BlockSpec((ng, 1, width), lambda g, i: (g, 0, 0))],
        out_specs=io_spec,
        out_shape=jax.ShapeDtypeStruct(u_g.shape, BF16),
        scratch_shapes=[pltpu.VMEM((ng, nch, tb, width), F32)] + [pltpu.VMEM((ng, nch, tb, half), F32)] * 4,
        compiler_params=_params("parallel", "parallel"),
        name="s5",
    )(u_g, toep, w_state, w_out, a_pow, d_lanes)


GROUPS_PER_BLOCK = LANES // SSM_GROUP
RUN_STEPS = LANES // SSM_GROUP
RUNS = S5_CHUNK // RUN_STEPS
REGROUP_WIDTH = RUN_STEPS * LANES
REGROUP_TOKENS = 512
STAGE_PAD = 8


def _selection_matrix():
    row = jnp.arange(REGROUP_WIDTH)
    step, rem = row // LANES, row % LANES
    col = (rem // SSM_GROUP) * LANES + step * SSM_GROUP + rem % SSM_GROUP
    return (col[:, None] == jnp.arange(REGROUP_WIDTH)[None, :]).astype(BF16)


def _to_groups_kernel(u_ref, sel_ref, o_ref, stage_ref, xcat_ref):
    tb, lc, _ = u_ref.shape
    nr = lc // S5_CHUNK
    slab = lc + STAGE_PAD
    for b in range(tb):
        stage_ref[b * slab:b * slab + lc, :] = u_ref[b].astype(F32)
    for r in range(nr):
        for s in range(S5_CHUNK):
            piece = stage_ref[pl.ds(r * S5_CHUNK + s, tb, stride=slab), :]
            run, step = divmod(s, RUN_STEPS)
            xcat_ref[run, r * tb:(r + 1) * tb, step * LANES:(step + 1) * LANES] = piece.astype(BF16)
    for run in range(RUNS):
        grouped = _dot(xcat_ref[run], sel_ref[...]).astype(BF16)
        for gam in range(GROUPS_PER_BLOCK):
            o_ref[gam, :, :, run * LANES:(run + 1) * LANES] = (
                grouped[:, gam * LANES:(gam + 1) * LANES].reshape(nr, tb, LANES))


def _to_tokens_kernel(z_ref, selt_ref, o_ref, stage_ref):
    _, nr, tb, _ = z_ref.shape
    lc = nr * S5_CHUNK
    slab = lc + STAGE_PAD
    for run in range(RUNS):
        zcat = jnp.concatenate([z_ref[gam, :, :, run * LANES:(run + 1) * LANES].reshape(nr * tb, LANES)
                                for gam in range(GROUPS_PER_BLOCK)], axis=-1)
        zt = _dot(zcat, selt_ref[...])
        for r in range(nr):
            for step in range(RUN_STEPS):
                t = run * RUN_STEPS + step
                stage_ref[pl.ds(r * S5_CHUNK + t, tb, stride=slab), :] = (
                    zt[r * tb:(r + 1) * tb, step * LANES:(step + 1) * LANES])
    for b in range(tb):
        o_ref[b] = stage_ref[b * slab:b * slab + lc, :].astype(BF16)


def _regroup_specs(b, seq_len):
    tb = S5_BATCH_TILE if b % S5_BATCH_TILE == 0 else b
    lc = REGROUP_TOKENS
    grid = (b // tb, seq_len // lc, SSM_WIDTH // LANES)
    token_spec = pl.BlockSpec((tb, lc, LANES), lambda i, c, j: (i, c, j))
    group_spec = pl.BlockSpec((GROUPS_PER_BLOCK, lc // S5_CHUNK, tb, S5_CHUNK * SSM_GROUP),
                              lambda i, c, j: (j, c, i, 0))
    stage = pltpu.VMEM((tb * (lc + STAGE_PAD), LANES), F32)
    return tb, grid, token_spec, group_spec, stage


def _to_group_major(u, sel, b, seq_len):
    tb, grid, token_spec, group_spec, stage = _regroup_specs(b, seq_len)
    return pl.pallas_call(
        _to_groups_kernel,
        grid=grid,
        in_specs=[token_spec, _resident(sel.shape)],
        out_specs=group_spec,
        out_shape=jax.ShapeDtypeStruct((SSM_GROUPS, seq_len // S5_CHUNK, b, S5_CHUNK * SSM_GROUP), BF16),
        scratch_shapes=[stage, pltpu.VMEM((RUNS, REGROUP_TOKENS // S5_CHUNK * tb, REGROUP_WIDTH), BF16)],
        compiler_params=_params("parallel", "parallel", "parallel"),
        name="to_groups",
    )(u, sel)


def _to_token_major(z_g, sel_t, b, seq_len):
    tb, grid, token_spec, group_spec, stage = _regroup_specs(b, seq_len)
    return pl.pallas_call(
        _to_tokens_kernel,
        grid=grid,
        in_specs=[group_spec, _resident(sel_t.shape)],
        out_specs=token_spec,
        out_shape=jax.ShapeDtypeStruct((b, seq_len, SSM_WIDTH), BF16),
        scratch_shapes=[stage],
        compiler_params=_params("parallel", "parallel", "parallel"),
        name="to_tokens",
    )(z_g, sel_t)


def _merge_ffn_kernel(x_ref, oret_ref, z_ref, gr_ref, gs_ref, wo_ref, wglu_ref, wout_ref, g2_ref, b2_ref,
                      wgu_ref, wdn_ref, g3_ref, b3_ref, o_ref, x2_ref):
    @pl.when(pl.program_id(0) == 0)
    def _():
        x2_ref[...] = jnp.zeros_like(x2_ref)

    x2 = x2_ref[...]
    o_ref[...] = _layer_norm(ALPHA * x2 + 0.5 * _swiglu(x2, wgu_ref, wdn_ref), g3_ref[...], b3_ref[...])

    oret = oret_ref[...]
    z = z_ref[...]
    pieces = []
    for lo in range(0, D_MODEL, MERGE_COLS):
        cols = slice(lo, lo + MERGE_COLS)
        y_ret = _dot(oret, wo_ref[:, cols])
        val = _dot(z, wglu_ref[:, cols])
        gate = _dot(z, wglu_ref[:, D_MODEL + lo:D_MODEL + lo + MERGE_COLS])
        y_ssm = val * _sigmoid(gate)
        pieces.append((gr_ref[:, cols].astype(F32) * y_ret + gs_ref[:, cols].astype(F32) * y_ssm).astype(BF16))
    mix = _dot(jnp.concatenate(pieces, axis=-1), wout_ref[...])
    x2_ref[...] = _layer_norm(ALPHA * x_ref[...] + mix, g2_ref[...], b2_ref[...])


def _merge_ffn(x, oret, z, gr, gs, layer, w_o, w_glu, w_out, ln2_g, ln2_b, w_gu, w_down, ln3_g, ln3_b, *,
               in_row0=0, rows=None):
    rows = x.shape[0] if rows is None else rows
    grid, in_map, out_spec = _delayed_specs(rows, in_row0 // TOKEN_TILE, 0, D_MODEL)
    tile = pl.BlockSpec((TOKEN_TILE, D_MODEL), in_map)
    weights = (w_o, w_glu, w_out, ln2_g, ln2_b, w_gu, w_down, ln3_g, ln3_b)
    return pl.pallas_call(
        _merge_ffn_kernel,
        grid=grid,
        in_specs=[tile] * 5 + [_layer(w, layer) for w in weights],
        out_specs=out_spec,
        out_shape=jax.ShapeDtypeStruct((rows, D_MODEL), F32),
        scratch_shapes=[pltpu.VMEM((TOKEN_TILE, D_MODEL), F32)],
        compiler_params=_params("arbitrary"),
        name="merge_ffn",
    )(x, oret, z, gr, gs, *weights)


def _trunk(xs, params):
    (ffn1_w_gu, ffn1_w_down, ln1_g, ln1_b, w_in, b_gate, ret_w_o, s5_lam_re, s5_lam_im, s5_log_dt,
     s5_b_re, s5_b_im, s5_c_re, s5_c_im, s5_d, s5_w_glu, w_out, ln2_g, ln2_b,
     ffn2_w_gu, ffn2_w_down, ln3_g, ln3_b) = params
    seq_len = xs[0].shape[1]
    assert all(x.shape[1:] == (seq_len, D_MODEL) for x in xs)
    assert seq_len % TOKEN_TILE == 0 and seq_len % RET_CHUNK == 0 and seq_len % REGROUP_TOKENS == 0
    sizes = [x.shape[0] * seq_len for x in xs]
    starts = [sum(sizes[:i]) for i in range(len(xs))]
    b, t = sum(x.shape[0] for x in xs), sum(sizes)
    cos_t, sin_t = _rotary_tables(seq_len)
    decay, edge = _retention_tables()
    sel = _selection_matrix()
    bf = lambda w: w.astype(BF16)
    row = lambda v: v.reshape(DEPTH, 1, v.shape[-1])
    ffn1 = (bf(ffn1_w_gu), bf(ffn1_w_down), row(ln1_g), row(ln1_b))
    w_in, b_gate = bf(w_in), row(b_gate)
    tail_weights = (bf(ret_w_o), bf(s5_w_glu), bf(w_out), row(ln2_g), row(ln2_b),
                    bf(ffn2_w_gu), bf(ffn2_w_down), row(ln3_g), row(ln3_b))
    x = None
    for l in range(DEPTH):
        if l == 0:
            for xi, size, start in zip(xs, sizes, starts):
                x = _ffn_ln(xi.reshape(size, D_MODEL), l, *ffn1, out_rows=t, out_row0=start, into=x)
        else:
            x = _ffn_ln(x, l, *ffn1)
        q, k, v, g, u, gr, gs = _in_proj(x, l, w_in, b_gate, cos_t, sin_t, seq_len)
        shape3 = lambda a: a.reshape(b, seq_len, a.shape[-1])
        oret = _retention(shape3(q), shape3(k), shape3(v), shape3(g), decay, edge)
        mats = _s5_matrices(s5_lam_re[l], s5_lam_im[l], s5_log_dt[l], s5_b_re[l], s5_b_im[l],
                            s5_c_re[l], s5_c_im[l], s5_d[l])
        z = _to_token_major(_s5(_to_group_major(shape3(u), sel, b, seq_len), *mats), sel.T, b, seq_len)
        tail = (x, oret.reshape(t, RET_V), z.reshape(t, SSM_WIDTH), gr, gs, l) + tail_weights
        if l < DEPTH - 1:
            x = _merge_ffn(*tail)
        else:
            ys = [_merge_ffn(*tail, in_row0=start, rows=size) for size, start in zip(sizes, starts)]
    return [y.reshape(xi.shape) for y, xi in zip(ys, xs)]


def kernel(x_prompt, x_sample, ffn1_w_gu, ffn1_w_down, ln1_g, ln1_b, w_in, b_gate, ret_w_o, s5_lam_re, s5_lam_im, s5_log_dt, s5_b_re, s5_b_im, s5_c_re, s5_c_im, s5_d, s5_w_glu, w_out, ln2_g, ln2_b, ffn2_w_gu, ffn2_w_down, ln3_g, ln3_b):
    params = (ffn1_w_gu, ffn1_w_down, ln1_g, ln1_b, w_in, b_gate, ret_w_o, s5_lam_re, s5_lam_im, s5_log_dt,
              s5_b_re, s5_b_im, s5_c_re, s5_c_im, s5_d, s5_w_glu, w_out, ln2_g, ln2_b,
              ffn2_w_gu, ffn2_w_down, ln3_g, ln3_b)
    y_prompt, y_sample = _trunk([x_prompt, x_sample], params)
    return (y_prompt, y_sample)
```

```python
import functools
import math

import jax
import jax.numpy as jnp
from jax import lax
from jax.experimental import pallas as pl
from jax.experimental.pallas import tpu as pltpu

F32 = jnp.float32
BF16 = jnp.bfloat16

D_MODEL = 1024
DEPTH = 2
RET_HEADS = 8
RET_QK_DIM = 64
RET_V_DIM = 128
RET_QK = RET_HEADS * RET_QK_DIM
RET_V = RET_HEADS * RET_V_DIM
ROPE_BASE = 10000.0
SSM_WIDTH = 1024
SSM_GROUP = 16
SSM_GROUPS = SSM_WIDTH // SSM_GROUP
SSM_STATE = 64
EIG_CLIP = -1e-4
D_FF = 2816
ALPHA = (2 * DEPTH) ** 0.25
LN_EPS = 1e-5

LANES = 128
MXU_DIM = 256
VMEM_LIMIT_BYTES = 56 * 1024 * 1024

TOKEN_TILE = 512
FF_CHUNKS = (1024, 1024, 768)
assert sum(FF_CHUNKS) == D_FF and all(w % MXU_DIM == 0 for w in FF_CHUNKS)
FFN_TILE = TOKEN_TILE
FFN_TILE_CHUNKS = FF_CHUNKS
MERGE_COLS = 1024
RET_CHUNK = MXU_DIM
S5_CHUNK = MXU_DIM // SSM_GROUP
S5_BATCH_TILE = 16
S5_GROUPS_PER_STEP = 2


def _resident(shape):
    nd = len(shape)
    return pl.BlockSpec(shape, lambda *_: (0,) * nd, pipeline_mode=pl.Buffered(1))


def _layer(stacked, layer):
    nd = stacked.ndim - 1
    return pl.BlockSpec((None,) + stacked.shape[1:], lambda *_: (layer,) + (0,) * nd,
                        pipeline_mode=pl.Buffered(1))


def _params(*sem):
    return pltpu.CompilerParams(dimension_semantics=sem, vmem_limit_bytes=VMEM_LIMIT_BYTES)


def _dot(a, b):
    return jnp.dot(a, b, preferred_element_type=F32)


def _sigmoid(x):
    return 1.0 / (1.0 + jnp.exp(-x))


def _layer_norm(y, g, b):
    mu = jnp.mean(y, axis=-1, keepdims=True)
    yc = y - mu
    var = jnp.mean(yc * yc, axis=-1, keepdims=True)
    return yc * lax.rsqrt(var + LN_EPS) * g + b


def _delayed_layer_norm(pre_ref, g_ref, b_ref, o_ref):
    @pl.when(pl.program_id(0) == 0)
    def _():
        pre_ref[...] = jnp.zeros_like(pre_ref)

    o_ref[...] = _layer_norm(pre_ref[...], g_ref[...], b_ref[...])


def _delayed_specs(rows, in_row0, out_row0, width, tile=TOKEN_TILE):
    n, in_t0, out_t0 = rows // tile, in_row0 // tile, out_row0 // tile
    assert rows % tile == 0 and in_row0 % tile == 0 and out_row0 % tile == 0
    in_map = lambda i: (jnp.minimum(i, n - 1) + in_t0, 0)
    out_map = lambda i: (jnp.maximum(i - 1, 0) + out_t0, 0)
    return (n + 1,), in_map, pl.BlockSpec((tile, width), out_map)


def _swiglu(x, wgu_ref, wdn_ref, chunks=FF_CHUNKS):
    xb = x.astype(BF16)
    acc = jnp.zeros(x.shape, F32)
    lo = 0
    for width in chunks:
        a = _dot(xb, wgu_ref[:, lo:lo + width])
        u = _dot(xb, wgu_ref[:, D_FF + lo:D_FF + lo + width])
        h = (a * _sigmoid(a) * u).astype(BF16)
        acc = acc + _dot(h, wdn_ref[lo:lo + width, :])
        lo += width
    return acc


def _ffn_ln_kernel(x_ref, wgu_ref, wdn_ref, g_ref, b_ref, *rest):
    o_ref, pre_ref = rest[-2:]
    _delayed_layer_norm(pre_ref, g_ref, b_ref, o_ref)
    x = x_ref[...]
    pre_ref[...] = ALPHA * x + 0.5 * _swiglu(x, wgu_ref, wdn_ref, FFN_TILE_CHUNKS)


def _ffn_ln(x, layer, w_gu, w_down, ln_g, ln_b, *, in_row0=0, rows=None, out_rows=None, out_row0=0, into=None):
    rows = x.shape[0] if rows is None else rows
    out_rows = rows if out_rows is None else out_rows
    grid, in_map, out_spec = _delayed_specs(rows, in_row0, out_row0, D_MODEL, FFN_TILE)
    args = [x, w_gu, w_down, ln_g, ln_b]
    in_specs = [pl.BlockSpec((FFN_TILE, D_MODEL), in_map)] + [_layer(a, layer) for a in args[1:]]
    aliases = {}
    if into is not None:
        args.append(into)
        in_specs.append(pl.BlockSpec(memory_space=pl.ANY))
        aliases = {len(args) - 1: 0}
        out_rows = into.shape[0]
    return pl.pallas_call(
        _ffn_ln_kernel,
        grid=grid,
        in_specs=in_specs,
        out_specs=out_spec,
        out_shape=jax.ShapeDtypeStruct((out_rows, D_MODEL), F32),
        input_output_aliases=aliases,
        scratch_shapes=[pltpu.VMEM((FFN_TILE, D_MODEL), F32)],
        compiler_params=_params("arbitrary"),
        name="ffn_ln",
    )(*args)


def _in_proj_kernel(x_ref, w_ref, bg_ref, cos_ref, sin_ref,
                    q_ref, k_ref, v_ref, g_ref, u_ref, gr_ref, gs_ref):
    xb = x_ref[...].astype(BF16)

    def seg(lo, width):
        return _dot(xb, w_ref[:, lo:lo + width])

    lane = lax.broadcasted_iota(jnp.int32, (1, RET_QK), 1)
    first_half = (lane % RET_QK_DIM) < (RET_QK_DIM // 2)
    cos = cos_ref[...]
    sin = sin_ref[...]

    def rotary(t):
        half = RET_QK_DIM // 2
        swapped = jnp.where(first_half, pltpu.roll(t, RET_QK - half, axis=1), pltpu.roll(t, half, axis=1))
        return t * cos + swapped * sin

    base = 2 * RET_QK
    gate0 = base + 2 * RET_V + SSM_WIDTH
    gr_ref[...] = _sigmoid(seg(gate0, D_MODEL) + bg_ref[:, :D_MODEL]).astype(BF16)
    gs_ref[...] = _sigmoid(seg(gate0 + D_MODEL, D_MODEL) + bg_ref[:, D_MODEL:]).astype(BF16)
    q_ref[...] = rotary(seg(0, RET_QK)).astype(BF16)
    k_ref[...] = (rotary(seg(RET_QK, RET_QK)) * (RET_QK_DIM ** -0.5)).astype(BF16)
    v_ref[...] = seg(base, RET_V).astype(BF16)
    g_ref[...] = seg(base + RET_V, RET_V).astype(BF16)
    u_ref[...] = seg(base + 2 * RET_V, SSM_WIDTH).astype(BF16)


def _in_proj(x, layer, w_in, b_gate, cos_t, sin_t, seq_len):
    t = x.shape[0]
    tiles_per_seq = seq_len // TOKEN_TILE
    tile = lambda w: pl.BlockSpec((TOKEN_TILE, w), lambda i: (i, 0))
    table = pl.BlockSpec((TOKEN_TILE, RET_QK), lambda i: (i % tiles_per_seq, 0))
    widths = (RET_QK, RET_QK, RET_V, RET_V, SSM_WIDTH, D_MODEL, D_MODEL)
    return pl.pallas_call(
        _in_proj_kernel,
        grid=(t // TOKEN_TILE,),
        in_specs=[tile(D_MODEL), _layer(w_in, layer), _layer(b_gate, layer), table, table],
        out_specs=[tile(w) for w in widths],
        out_shape=[jax.ShapeDtypeStruct((t, w), BF16) for w in widths],
        compiler_params=_params("parallel"),
        name="in_proj",
    )(x, w_in, b_gate, cos_t, sin_t)


def _rotary_tables(seq_len):
    half = RET_QK_DIM // 2
    inv_freq = ROPE_BASE ** (-jnp.arange(half, dtype=F32) / half)
    ang = jnp.arange(seq_len, dtype=F32)[:, None] * inv_freq[None, :]
    cos = jnp.cos(ang)
    sin = jnp.sin(ang)
    cos_t = jnp.tile(jnp.concatenate([cos, cos], axis=-1), (1, RET_HEADS))
    sin_t = jnp.tile(jnp.concatenate([-sin, sin], axis=-1), (1, RET_HEADS))
    return cos_t, sin_t


def _retention_tables():
    c = RET_CHUNK
    log_gamma = jnp.log1p(-jnp.exp2(-5.0 - jnp.arange(RET_HEADS, dtype=F32)))
    idx = jnp.arange(c, dtype=F32)
    dist = jnp.abs(idx[:, None] - idx[None, :])
    decay = jnp.exp(log_gamma[:, None, None] * dist[None])
    expo = jnp.stack([idx + 1.0, c - idx, c - 1.0 - idx, idx], axis=0)
    edge = jnp.exp(log_gamma[:, None, None] * expo[None])
    edge = jnp.broadcast_to(edge[..., None], (RET_HEADS, 4, c, LANES))
    return decay, edge


def _retention_kernel(q_ref, k_ref, v_ref, g_ref, decay_ref, edge_ref, o_ref, kvf_ref, kvb_ref):
    c = RET_CHUNK
    nc = q_ref.shape[1] // c
    lane = lax.broadcasted_iota(jnp.int32, (1, LANES), 1)
    contract_rows = (((0,), (0,)), ((), ()))
    contract_lanes = (((1,), (1,)), ((), ()))
    for e in range(2):
        mine = (lane < RET_QK_DIM) if e == 0 else (lane >= RET_QK_DIM)
        vsl = slice(e * RET_V_DIM, (e + 1) * RET_V_DIM)
        xi_f, xi_b = edge_ref[e, 0], edge_ref[e, 1]
        zeta_f, zeta_b = edge_ref[e, 2], edge_ref[e, 3]
        chunk_decay = xi_f[c - 1:c, :]

        def k_masked(n):
            kc = k_ref[0, n * c:(n + 1) * c, :]
            return jnp.where(mine, kc, jnp.zeros_like(kc))

        state = jnp.zeros((LANES, RET_V_DIM), F32)
        for n in range(nc):
            kvf_ref[n] = state
            if n + 1 < nc:
                kz = (k_masked(n).astype(F32) * zeta_f).astype(BF16)
                kv = lax.dot_general(kz, v_ref[0, n * c:(n + 1) * c, vsl], contract_rows,
                                     preferred_element_type=F32)
                state = chunk_decay * state + kv
        state = jnp.zeros((LANES, RET_V_DIM), F32)
        for n in range(nc - 1, -1, -1):
            kvb_ref[n] = state
            if n > 0:
                kz = (k_masked(n).astype(F32) * zeta_b).astype(BF16)
                kv = lax.dot_general(kz, v_ref[0, n * c:(n + 1) * c, vsl], contract_rows,
                                     preferred_element_type=F32)
                state = chunk_decay * state + kv

        for n in range(nc):
            rows = slice(n * c, (n + 1) * c)
            qc = q_ref[0, rows, :]
            vc = v_ref[0, rows, vsl]
            scores = lax.dot_general(qc, k_masked(n), contract_lanes, preferred_element_type=F32)
            scores = scores * decay_ref[e]
            o = _dot(scores.astype(BF16), vc)
            qf = qc.astype(F32)
            o = o + _dot((qf * xi_f).astype(BF16), kvf_ref[n].astype(BF16))
            o = o + _dot((qf * xi_b).astype(BF16), kvb_ref[n].astype(BF16))
            mu = jnp.mean(o, axis=-1, keepdims=True)
            oc = o - mu
            var = jnp.mean(oc * oc, axis=-1, keepdims=True)
            on = oc * lax.rsqrt(var + LN_EPS)
            gate = g_ref[0, rows, vsl].astype(F32)
            o_ref[0, rows, vsl] = (gate * _sigmoid(gate) * on).astype(BF16)


def _retention(q, k, v, g, decay, edge):
    b, seq_len, _ = q.shape
    nc = seq_len // RET_CHUNK
    qk_spec = pl.BlockSpec((1, seq_len, LANES), lambda i, h: (i, 0, h))
    v_spec = pl.BlockSpec((1, seq_len, 2 * RET_V_DIM), lambda i, h: (i, 0, h))
    return pl.pallas_call(
        _retention_kernel,
        grid=(b, RET_HEADS // 2),
        in_specs=[qk_spec, qk_spec, v_spec, v_spec,
                  pl.BlockSpec((2, RET_CHUNK, RET_CHUNK), lambda i, h: (h, 0, 0)),
                  pl.BlockSpec((2, 4, RET_CHUNK, LANES), lambda i, h: (h, 0, 0, 0))],
        out_specs=v_spec,
        out_shape=jax.ShapeDtypeStruct((b, seq_len, RET_V), BF16),
        scratch_shapes=[pltpu.VMEM((nc, LANES, RET_V_DIM), F32), pltpu.VMEM((nc, LANES, RET_V_DIM), F32)],
        compiler_params=_params("parallel", "parallel"),
        name="retention",
    )(q, k, v, g, decay, edge)


def _s5_matrices(lam_re, lam_im, log_dt, b_re, b_im, c_re, c_im, d_skip):
    hi = lax.Precision.HIGHEST
    cs, n, p = S5_CHUNK, SSM_STATE, SSM_GROUP
    lr = jnp.minimum(lam_re.astype(F32), EIG_CLIP)
    li = lam_im.astype(F32)
    dt = jnp.exp(log_dt.astype(F32))[..., None]
    groups, width = SSM_GROUPS, cs * p
    steps = jnp.arange(cs + 1, dtype=F32)[None, None, :, None]
    mag = jnp.exp((lr * dt)[:, :, None, :] * steps)
    ang = (li * dt)[:, :, None, :] * steps
    pw_re = mag * jnp.cos(ang)
    pw_im = mag * jnp.sin(ang)
    ab_re, ab_im = pw_re[:, :, 1], pw_im[:, :, 1]
    den = lr * lr + li * li
    num_re = ab_re - 1.0
    f_re = ((num_re * lr + ab_im * li) / den)[:, :, None, :]
    f_im = ((ab_im * lr - num_re * li) / den)[:, :, None, :]
    br_t = b_re.astype(F32).transpose(0, 1, 3, 2)
    bi_t = b_im.astype(F32).transpose(0, 1, 3, 2)
    bb_re = f_re * br_t - f_im * bi_t
    bb_im = f_re * bi_t + f_im * br_t
    cr_t = c_re.astype(F32).transpose(0, 1, 3, 2)[:, :, :, None, :]
    ci_t = c_im.astype(F32).transpose(0, 1, 3, 2)[:, :, :, None, :]
    pwn_re = pw_re.transpose(0, 1, 3, 2)[..., None]
    pwn_im = pw_im.transpose(0, 1, 3, 2)[..., None]
    cp_re = cr_t * pwn_re - ci_t * pwn_im
    cp_im = cr_t * pwn_im + ci_t * pwn_re

    lhs = jnp.concatenate([bb_re, -bb_im], axis=-1)
    rhs = jnp.concatenate([cp_re[:, :, :, :cs], cp_im[:, :, :, :cs]], axis=2).reshape(2, groups, 2 * n, width)
    taps = jnp.einsum('dgin,dgnl->dgil', lhs, rhs, precision=hi)
    zeros = jnp.zeros((groups, p, width), F32)
    pad_f = jnp.concatenate([zeros, taps[0]], axis=-1)
    pad_b = jnp.concatenate([taps[1].reshape(groups, p, cs, p)[:, :, ::-1].reshape(groups, p, width), zeros], axis=-1)
    toep = jnp.stack([pad_f[:, :, width - p * s:2 * width - p * s]
                      + pad_b[:, :, p * (cs - 1 - s):p * (cs - 1 - s) + width] for s in range(cs)], axis=1)
    toep = toep.reshape(groups, width, width)

    def state_cols(pw_r, pw_i, bbr, bbi):
        pw_r, pw_i, bbr, bbi = pw_r[:, :, None, :], pw_i[:, :, None, :], bbr[:, None], bbi[:, None]
        shape = (groups, width, n)
        return (pw_r * bbr - pw_i * bbi).reshape(shape), (pw_r * bbi + pw_i * bbr).reshape(shape)

    sf_re, sf_im = state_cols(pw_re[0, :, :cs][:, ::-1], pw_im[0, :, :cs][:, ::-1], bb_re[0], bb_im[0])
    sb_re, sb_im = state_cols(pw_re[1, :, :cs], pw_im[1, :, :cs], bb_re[1], bb_im[1])
    w_state = jnp.concatenate([sf_re, sb_re, sf_im, sb_im], axis=-1)

    rows = lambda a: a.reshape(groups, n, width)
    w_out = jnp.concatenate([rows(cp_re[0, :, :, 1:]), rows(cp_re[1, :, :, 1:][:, :, ::-1]),
                             rows(-cp_im[0, :, :, 1:]), rows(-cp_im[1, :, :, 1:][:, :, ::-1])], axis=1)

    a_re = jnp.concatenate([pw_re[0, :, cs], pw_re[1, :, cs]], axis=-1)
    a_im = jnp.concatenate([pw_im[0, :, cs], pw_im[1, :, cs]], axis=-1)
    a_pow = jnp.stack([a_re, a_im], axis=1)
    d_lanes = jnp.tile(d_skip.astype(F32).reshape(SSM_GROUPS, 1, p), (1, 1, cs))
    return toep.astype(BF16), w_state.astype(BF16), w_out.astype(BF16), a_pow, d_lanes


def _s5_kernel(u_ref, toep_ref, wst_ref, wout_ref, apow_ref, d_ref, z_ref,
               loc_ref, fwd_re_ref, fwd_im_ref, bwd_re_ref, bwd_im_ref):
    ng, nch, tb, width = u_ref.shape
    for gi in range(ng):
        loc_ref[gi] = _dot(u_ref[gi].reshape(nch * tb, width), wst_ref[gi]).reshape(nch, tb, width)

    half = width // 2
    lane = lax.broadcasted_iota(jnp.int32, (1, half), 1)
    is_fwd = lane < SSM_STATE

    def step(i, carry):
        j = nch - 1 - i
        out = []
        for gi in range(ng):
            h_re, h_im = carry[gi]
            a_re = apow_ref[gi, 0:1, :]
            a_im = apow_ref[gi, 1:2, :]
            fwd_re_ref[gi, i] = h_re
            fwd_im_ref[gi, i] = h_im
            bwd_re_ref[gi, j] = h_re
            bwd_im_ref[gi, j] = h_im
            s_re = jnp.where(is_fwd, loc_ref[gi, i, :, :half], loc_ref[gi, j, :, :half])
            s_im = jnp.where(is_fwd, loc_ref[gi, i, :, half:], loc_ref[gi, j, :, half:])
            out.append((a_re * h_re - a_im * h_im + s_re, a_re * h_im + a_im * h_re + s_im))
        return tuple(out)

    zero = jnp.zeros((tb, half), F32)
    lax.fori_loop(0, nch, step, tuple((zero, zero) for _ in range(ng)))

    for gi in range(ng):
        u = u_ref[gi].reshape(nch * tb, width)
        st_re = jnp.where(is_fwd, fwd_re_ref[gi], bwd_re_ref[gi]).reshape(nch * tb, half)
        st_im = jnp.where(is_fwd, fwd_im_ref[gi], bwd_im_ref[gi]).reshape(nch * tb, half)
        state = jnp.concatenate([st_re, st_im], axis=-1).astype(BF16)
        y = _dot(u, toep_ref[gi]) + _dot(state, wout_ref[gi]) + u.astype(F32) * d_ref[gi]
        gelu = 0.5 * y * (1.0 + jnp.tanh(math.sqrt(2.0 / math.pi) * (y + 0.044715 * (y * y * y))))
        z_ref[gi] = gelu.astype(BF16).reshape(nch, tb, width)


def _s5(u_g, toep, w_state, w_out, a_pow, d_lanes):
    groups, nch, b, width = u_g.shape
    tb = S5_BATCH_TILE if b % S5_BATCH_TILE == 0 else b
    ng = S5_GROUPS_PER_STEP
    io_spec = pl.BlockSpec((ng, nch, tb, width), lambda g, i: (g, 0, i, 0))
    mat_spec = pl.BlockSpec((ng, width, width), lambda g, i: (g, 0, 0))
    half = width // 2
    return pl.pallas_call(
        _s5_kernel,
        grid=(groups // ng, b // tb),
        in_specs=[io_spec, mat_spec, mat_spec, mat_spec,
                  pl.BlockSpec((ng, 2, half), lambda g, i: (g, 0, 0)),
                  pl.BlockSpec((ng, 1, width), lambda g, i: (g, 0, 0))],
        out_specs=io_spec,
        out_shape=jax.ShapeDtypeStruct(u_g.shape, BF16),
        scratch_shapes=[pltpu.VMEM((ng, nch, tb, width), F32)] + [pltpu.VMEM((ng, nch, tb, half), F32)] * 4,
        compiler_params=_params("parallel", "parallel"),
        name="s5",
    )(u_g, toep, w_state, w_out, a_pow, d_lanes)


GROUPS_PER_BLOCK = LANES // SSM_GROUP
RUN_STEPS = LANES // SSM_GROUP
RUNS = S5_CHUNK // RUN_STEPS
REGROUP_WIDTH = RUN_STEPS * LANES
REGROUP_TOKENS = 1024
STAGE_PAD = 8


def _selection_matrix():
    row = jnp.arange(REGROUP_WIDTH)
    step, rem = row // LANES, row % LANES
    col = (rem // SSM_GROUP) * LANES + step * SSM_GROUP + rem % SSM_GROUP
    return (col[:, None] == jnp.arange(REGROUP_WIDTH)[None, :]).astype(BF16)


def _to_groups_kernel(u_ref, sel_ref, o_ref, stage_ref, xcat_ref):
    tb, lc, _ = u_ref.shape
    nr = lc // S5_CHUNK
    slab = lc + STAGE_PAD
    for b in range(tb):
        stage_ref[b * slab:b * slab + lc, :] = u_ref[b].astype(F32)
    for r in range(nr):
        for s in range(S5_CHUNK):
            piece = stage_ref[pl.ds(r * S5_CHUNK + s, tb, stride=slab), :]
            run, step = divmod(s, RUN_STEPS)
            xcat_ref[run, r * tb:(r + 1) * tb, step * LANES:(step + 1) * LANES] = piece.astype(BF16)
    for run in range(RUNS):
        grouped = _dot(xcat_ref[run], sel_ref[...]).astype(BF16)
        for gam in range(GROUPS_PER_BLOCK):
            o_ref[gam, :, :, run * LANES:(run + 1) * LANES] = (
                grouped[:, gam * LANES:(gam + 1) * LANES].reshape(nr, tb, LANES))


def _to_tokens_kernel(z_ref, selt_ref, o_ref, stage_ref):
    _, nr, tb, _ = z_ref.shape
    lc = nr * S5_CHUNK
    slab = lc + STAGE_PAD
    for run in range(RUNS):
        zcat = jnp.concatenate([z_ref[gam, :, :, run * LANES:(run + 1) * LANES].reshape(nr * tb, LANES)
                                for gam in range(GROUPS_PER_BLOCK)], axis=-1)
        zt = _dot(zcat, selt_ref[...])
        for r in range(nr):
            for step in range(RUN_STEPS):
                t = run * RUN_STEPS + step
                stage_ref[pl.ds(r * S5_CHUNK + t, tb, stride=slab), :] = (
                    zt[r * tb:(r + 1) * tb, step * LANES:(step + 1) * LANES])
    for b in range(tb):
        o_ref[b] = stage_ref[b * slab:b * slab + lc, :].astype(BF16)


def _regroup_specs(b, seq_len):
    tb = S5_BATCH_TILE if b % S5_BATCH_TILE == 0 else b
    lc = REGROUP_TOKENS
    grid = (b // tb, seq_len // lc, SSM_WIDTH // LANES)
    token_spec = pl.BlockSpec((tb, lc, LANES), lambda i, c, j: (i, c, j))
    group_spec = pl.BlockSpec((GROUPS_PER_BLOCK, lc // S5_CHUNK, tb, S5_CHUNK * SSM_GROUP),
                              lambda i, c, j: (j, c, i, 0))
    stage = pltpu.VMEM((tb * (lc + STAGE_PAD), LANES), F32)
    return tb, grid, token_spec, group_spec, stage


def _to_group_major(u, sel, b, seq_len):
    tb, grid, token_spec, group_spec, stage = _regroup_specs(b, seq_len)
    return pl.pallas_call(
        _to_groups_kernel,
        grid=grid,
        in_specs=[token_spec, _resident(sel.shape)],
        out_specs=group_spec,
        out_shape=jax.ShapeDtypeStruct((SSM_GROUPS, seq_len // S5_CHUNK, b, S5_CHUNK * SSM_GROUP), BF16),
        scratch_shapes=[stage, pltpu.VMEM((RUNS, REGROUP_TOKENS // S5_CHUNK * tb, REGROUP_WIDTH), BF16)],
        compiler_params=_params("parallel", "parallel", "parallel"),
        name="to_groups",
    )(u, sel)


def _to_token_major(z_g, sel_t, b, seq_len):
    tb, grid, token_spec, group_spec, stage = _regroup_specs(b, seq_len)
    return pl.pallas_call(
        _to_tokens_kernel,
        grid=grid,
        in_specs=[group_spec, _resident(sel_t.shape)],
        out_specs=token_spec,
        out_shape=jax.ShapeDtypeStruct((b, seq_len, SSM_WIDTH), BF16),
        scratch_shapes=[stage],
        compiler_params=_params("parallel", "parallel", "parallel"),
        name="to_tokens",
    )(z_g, sel_t)


def _merge_ffn_kernel(x_ref, oret_ref, z_ref, gr_ref, gs_ref, wo_ref, wglu_ref, wout_ref, g2_ref, b2_ref,
                      wgu_ref, wdn_ref, g3_ref, b3_ref, o_ref, x2_ref):
    @pl.when(pl.program_id(0) == 0)
    def _():
        x2_ref[...] = jnp.zeros_like(x2_ref)

    x2 = x2_ref[...]
    o_ref[...] = _layer_norm(ALPHA * x2 + 0.5 * _swiglu(x2, wgu_ref, wdn_ref), g3_ref[...], b3_ref[...])

    oret = oret_ref[...]
    z = z_ref[...]
    pieces = []
    for lo in range(0, D_MODEL, MERGE_COLS):
        cols = slice(lo, lo + MERGE_COLS)
        y_ret = _dot(oret, wo_ref[:, cols])
        val = _dot(z, wglu_ref[:, cols])
        gate = _dot(z, wglu_ref[:, D_MODEL + lo:D_MODEL + lo + MERGE_COLS])
        y_ssm = val * _sigmoid(gate)
        pieces.append((gr_ref[:, cols].astype(F32) * y_ret + gs_ref[:, cols].astype(F32) * y_ssm).astype(BF16))
    mix = _dot(jnp.concatenate(pieces, axis=-1), wout_ref[...])
    x2_ref[...] = _layer_norm(ALPHA * x_ref[...] + mix, g2_ref[...], b2_ref[...])


def _merge_ffn(x, oret, z, gr, gs, layer, w_o, w_glu, w_out, ln2_g, ln2_b, w_gu, w_down, ln3_g, ln3_b, *,
               in_row0=0, rows=None):
    rows = x.shape[0] if rows is None else rows
    grid, in_map, out_spec = _delayed_specs(rows, in_row0, 0, D_MODEL)
    tile = pl.BlockSpec((TOKEN_TILE, D_MODEL), in_map)
    weights = (w_o, w_glu, w_out, ln2_g, ln2_b, w_gu, w_down, ln3_g, ln3_b)
    return pl.pallas_call(
        _merge_ffn_kernel,
        grid=grid,
        in_specs=[tile] * 5 + [_layer(w, layer) for w in weights],
        out_specs=out_spec,
        out_shape=jax.ShapeDtypeStruct((rows, D_MODEL), F32),
        scratch_shapes=[pltpu.VMEM((TOKEN_TILE, D_MODEL), F32)],
        compiler_params=_params("arbitrary"),
        name="merge_ffn",
    )(x, oret, z, gr, gs, *weights)


def _trunk(xs, params):
    (ffn1_w_gu, ffn1_w_down, ln1_g, ln1_b, w_in, b_gate, ret_w_o, s5_lam_re, s5_lam_im, s5_log_dt,
     s5_b_re, s5_b_im, s5_c_re, s5_c_im, s5_d, s5_w_glu, w_out, ln2_g, ln2_b,
     ffn2_w_gu, ffn2_w_down, ln3_g, ln3_b) = params
    seq_len = xs[0].shape[1]
    assert all(x.shape[1:] == (seq_len, D_MODEL) for x in xs)
    assert seq_len % TOKEN_TILE == 0 and seq_len % RET_CHUNK == 0 and seq_len % REGROUP_TOKENS == 0
    sizes = [x.shape[0] * seq_len for x in xs]
    starts = [sum(sizes[:i]) for i in range(len(xs))]
    b, t = sum(x.shape[0] for x in xs), sum(sizes)
    cos_t, sin_t = _rotary_tables(seq_len)
    decay, edge = _retention_tables()
    sel = _selection_matrix()
    bf = lambda w: w.astype(BF16)
    row = lambda v: v.reshape(DEPTH, 1, v.shape[-1])
    ffn1 = (bf(ffn1_w_gu), bf(ffn1_w_down), row(ln1_g), row(ln1_b))
    w_in, b_gate = bf(w_in), row(b_gate)
    tail_weights = (bf(ret_w_o), bf(s5_w_glu), bf(w_out), row(ln2_g), row(ln2_b),
                    bf(ffn2_w_gu), bf(ffn2_w_down), row(ln3_g), row(ln3_b))
    x = None
    for l in range(DEPTH):
        if l == 0:
            for xi, size, start in zip(xs, sizes, starts):
                x = _ffn_ln(xi.reshape(size, D_MODEL), l, *ffn1, out_rows=t, out_row0=start, into=x)
        else:
            x = _ffn_ln(x, l, *ffn1)
        q, k, v, g, u, gr, gs = _in_proj(x, l, w_in, b_gate, cos_t, sin_t, seq_len)
        shape3 = lambda a: a.reshape(b, seq_len, a.shape[-1])
        oret = _retention(shape3(q), shape3(k), shape3(v), shape3(g), decay, edge)
        mats = _s5_matrices(s5_lam_re[l], s5_lam_im[l], s5_log_dt[l], s5_b_re[l], s5_b_im[l],
                            s5_c_re[l], s5_c_im[l], s5_d[l])
        z = _to_token_major(_s5(_to_group_major(shape3(u), sel, b, seq_len), *mats), sel.T, b, seq_len)
        tail = (x, oret.reshape(t, RET_V), z.reshape(t, SSM_WIDTH), gr, gs, l) + tail_weights
        if l < DEPTH - 1:
            x = _merge_ffn(*tail)
        else:
            ys = [_merge_ffn(*tail, in_row0=start, rows=size) for size, start in zip(sizes, starts)]
    return [y.reshape(xi.shape) for y, xi in zip(ys, xs)]


def kernel(x_prompt, x_sample, ffn1_w_gu, ffn1_w_down, ln1_g, ln1_b, w_in, b_gate, ret_w_o, s5_lam_re, s5_lam_im, s5_log_dt, s5_b_re, s5_b_im, s5_c_re, s5_c_im, s5_d, s5_w_glu, w_out, ln2_g, ln2_b, ffn2_w_gu, ffn2_w_down, ln3_g, ln3_b):
    params = (ffn1_w_gu, ffn1_w_down, ln1_g, ln1_b, w_in, b_gate, ret_w_o, s5_lam_re, s5_lam_im, s5_log_dt,
              s5_b_re, s5_b_im, s5_c_re, s5_c_im, s5_d, s5_w_glu, w_out, ln2_g, ln2_b,
              ffn2_w_gu, ffn2_w_down, ln3_g, ln3_b)
    y_prompt, y_sample = _trunk([x_prompt, x_sample], params)
    return (y_prompt, y_sample)
```

```python
import math

import jax
import jax.numpy as jnp
from jax import lax
from jax.experimental import pallas as pl
from jax.experimental.pallas import tpu as pltpu

F32 = jnp.float32
BF16 = jnp.bfloat16

D_MODEL = 1024
DEPTH = 2
RET_HEADS = 8
RET_QK_DIM = 64
RET_V_DIM = 128
RET_QK = RET_HEADS * RET_QK_DIM
RET_V = RET_HEADS * RET_V_DIM
ROPE_BASE = 10000.0
SSM_WIDTH = 1024
SSM_GROUP = 16
SSM_GROUPS = SSM_WIDTH // SSM_GROUP
SSM_STATE = 64
EIG_CLIP = -1e-4
D_FF = 2816
ALPHA = (2 * DEPTH) ** 0.25
LN_EPS = 1e-5

LANES = 128
SUBLANES = 8
MXU_DIM = 256
VMEM_LIMIT_BYTES = 56 * 1024 * 1024

TOKEN_TILE = 512
FF_CHUNKS = (1536, 1280)
assert sum(FF_CHUNKS) == D_FF and all(w % MXU_DIM == 0 for w in FF_CHUNKS)
RET_CHUNK = MXU_DIM
S5_CHUNK = MXU_DIM // SSM_GROUP
S5_BATCH_TILE = 16
S5_GROUPS_PER_STEP = 4


def _resident(shape):
    nd = len(shape)
    return pl.BlockSpec(shape, lambda *_: (0,) * nd, pipeline_mode=pl.Buffered(1))


def _layer(stacked, layer):
    nd = stacked.ndim - 1
    return pl.BlockSpec((None,) + stacked.shape[1:], lambda *_: (layer,) + (0,) * nd,
                        pipeline_mode=pl.Buffered(1))


def _params(*sem):
    return pltpu.CompilerParams(dimension_semantics=sem, vmem_limit_bytes=VMEM_LIMIT_BYTES)


def _dot(a, b):
    return jnp.dot(a, b, preferred_element_type=F32)


def _sigmoid(x):
    return 1.0 / (1.0 + jnp.exp(-x))


def _layer_norm(y, g, b):
    mu = jnp.mean(y, axis=-1, keepdims=True)
    yc = y - mu
    var = jnp.mean(yc * yc, axis=-1, keepdims=True)
    return yc * lax.rsqrt(var + LN_EPS) * g + b


def _delayed_layer_norm(pre_ref, g_ref, b_ref, o_ref):
    @pl.when(pl.program_id(0) == 0)
    def _():
        pre_ref[...] = jnp.zeros_like(pre_ref)

    o_ref[...] = _layer_norm(pre_ref[...], g_ref[...], b_ref[...])


def _delayed_specs(rows, in_row0, out_row0, width):
    tile = TOKEN_TILE
    n, in_t0, out_t0 = rows // tile, in_row0 // tile, out_row0 // tile
    assert rows % tile == 0 and in_row0 % tile == 0 and out_row0 % tile == 0
    in_map = lambda i: (jnp.minimum(i, n - 1) + in_t0, 0)
    out_map = lambda i: (jnp.maximum(i - 1, 0) + out_t0, 0)
    return (n + 1,), in_map, pl.BlockSpec((tile, width), out_map)


def _swiglu(x, wgu_ref, wdn_ref):
    xb = x.astype(BF16)
    acc = jnp.zeros(x.shape, F32)
    lo = 0
    for width in FF_CHUNKS:
        a = _dot(xb, wgu_ref[:, lo:lo + width])
        u = _dot(xb, wgu_ref[:, D_FF + lo:D_FF + lo + width])
        h = (a * _sigmoid(a) * u).astype(BF16)
        acc = acc + _dot(h, wdn_ref[lo:lo + width, :])
        lo += width
    return acc


def _ffn_ln_kernel(x_ref, wgu_ref, wdn_ref, g_ref, b_ref, *rest):
    o_ref, pre_ref = rest[-2:]
    _delayed_layer_norm(pre_ref, g_ref, b_ref, o_ref)
    x = x_ref[...]
    pre_ref[...] = ALPHA * x + 0.5 * _swiglu(x, wgu_ref, wdn_ref)


def _ffn_ln(x, layer, w_gu, w_down, ln_g, ln_b, *, in_row0=0, rows=None, out_rows=None, out_row0=0, into=None):
    rows = x.shape[0] if rows is None else rows
    out_rows = rows if out_rows is None else out_rows
    grid, in_map, out_spec = _delayed_specs(rows, in_row0, out_row0, D_MODEL)
    args = [x, w_gu, w_down, ln_g, ln_b]
    in_specs = [pl.BlockSpec((TOKEN_TILE, D_MODEL), in_map)] + [_layer(a, layer) for a in args[1:]]
    aliases = {}
    if into is not None:
        args.append(into)
        in_specs.append(pl.BlockSpec(memory_space=pl.ANY))
        aliases = {len(args) - 1: 0}
        out_rows = into.shape[0]
    return pl.pallas_call(
        _ffn_ln_kernel,
        grid=grid,
        in_specs=in_specs,
        out_specs=out_spec,
        out_shape=jax.ShapeDtypeStruct((out_rows, D_MODEL), F32),
        input_output_aliases=aliases,
        scratch_shapes=[pltpu.VMEM((TOKEN_TILE, D_MODEL), F32)],
        compiler_params=_params("arbitrary"),
        name="ffn_ln",
    )(*args)


def _in_proj_kernel(x_ref, w_ref, bg_ref, cos_ref, sin_ref, edge_ref,
                    q_ref, k_ref, v_ref, g_ref, u_ref, gr_ref, gs_ref):
    xb = x_ref[...].astype(BF16)

    def seg(lo, width):
        return _dot(xb, w_ref[:, lo:lo + width])

    lane = lax.broadcasted_iota(jnp.int32, (1, RET_QK), 1)
    first_half = (lane % RET_QK_DIM) < (RET_QK_DIM // 2)
    cos = cos_ref[...]
    sin = sin_ref[...]

    def rotary(t):
        half = RET_QK_DIM // 2
        swapped = jnp.where(first_half, pltpu.roll(t, RET_QK - half, axis=1), pltpu.roll(t, half, axis=1))
        return t * cos + swapped * sin

    base = 2 * RET_QK
    gate0 = base + 2 * RET_V + SSM_WIDTH
    gr_ref[...] = _sigmoid(seg(gate0, D_MODEL) + bg_ref[:, :D_MODEL]).astype(BF16)
    gs_ref[...] = _sigmoid(seg(gate0 + D_MODEL, D_MODEL) + bg_ref[:, D_MODEL:]).astype(BF16)
    q = rotary(seg(0, RET_QK))
    k = rotary(seg(RET_QK, RET_QK)) * (RET_QK_DIM ** -0.5)
    for form in range(3):
        cols = slice(form * RET_QK, (form + 1) * RET_QK)
        if form == 0:
            q_ref[:, cols] = q.astype(BF16)
            k_ref[:, cols] = k.astype(BF16)
        else:
            q_ref[:, cols] = (q * edge_ref[:, (form - 1) * RET_QK:form * RET_QK]).astype(BF16)
            k_ref[:, cols] = (k * edge_ref[:, (form + 1) * RET_QK:(form + 2) * RET_QK]).astype(BF16)
    v_ref[...] = seg(base, RET_V).astype(BF16)
    g_ref[...] = seg(base + RET_V, RET_V).astype(BF16)
    u_ref[...] = seg(base + 2 * RET_V, SSM_WIDTH).astype(BF16)


def _in_proj(x, layer, w_in, b_gate, cos_t, sin_t, edge_tok, seq_len):
    t = x.shape[0]
    tiles_per_seq = seq_len // TOKEN_TILE
    tile = lambda w: pl.BlockSpec((TOKEN_TILE, w), lambda i: (i, 0))
    table = pl.BlockSpec((TOKEN_TILE, RET_QK), lambda i: (i % tiles_per_seq, 0))
    widths = (3 * RET_QK, 3 * RET_QK, RET_V, RET_V, SSM_WIDTH, D_MODEL, D_MODEL)
    return pl.pallas_call(
        _in_proj_kernel,
        grid=(t // TOKEN_TILE,),
        in_specs=[tile(D_MODEL), _layer(w_in, layer), _layer(b_gate, layer), table, table,
                  _resident(edge_tok.shape)],
        out_specs=[tile(w) for w in widths],
        out_shape=[jax.ShapeDtypeStruct((t, w), BF16) for w in widths],
        compiler_params=_params("parallel"),
        name="in_proj",
    )(x, w_in, b_gate, cos_t, sin_t, edge_tok)


def _rotary_tables(seq_len):
    half = RET_QK_DIM // 2
    inv_freq = ROPE_BASE ** (-jnp.arange(half, dtype=F32) / half)
    ang = jnp.arange(seq_len, dtype=F32)[:, None] * inv_freq[None, :]
    cos = jnp.cos(ang)
    sin = jnp.sin(ang)
    cos_t = jnp.tile(jnp.concatenate([cos, cos], axis=-1), (1, RET_HEADS))
    sin_t = jnp.tile(jnp.concatenate([-sin, sin], axis=-1), (1, RET_HEADS))
    return cos_t, sin_t


def _retention_tables():
    c = RET_CHUNK
    log_gamma = jnp.log1p(-jnp.exp2(-5.0 - jnp.arange(RET_HEADS, dtype=F32)))
    idx = jnp.arange(c, dtype=F32)
    dist = jnp.abs(idx[:, None] - idx[None, :])
    decay = jnp.exp(log_gamma[:, None, None] * dist[None])
    expo = jnp.stack([idx + 1.0, c - idx, c - 1.0 - idx, idx], axis=0)
    edge = jnp.exp(log_gamma[:, None, None] * expo[None])
    edge_tok = jnp.tile(edge.transpose(2, 1, 0), (TOKEN_TILE // c, 1, 1))
    edge_tok = jnp.repeat(edge_tok, RET_QK_DIM, axis=-1).reshape(TOKEN_TILE, 4 * RET_QK)
    chunk_decay = jnp.broadcast_to(jnp.exp(c * log_gamma)[:, None, None], (RET_HEADS, SUBLANES, LANES))
    return decay, edge_tok, chunk_decay


def _retention_kernel(q_ref, qf_ref, qb_ref, k_ref, kf_ref, kb_ref, v_ref, g_ref, decay_ref, cd_ref, o_ref,
                      kvf_ref, kvb_ref):
    c = RET_CHUNK
    nc = q_ref.shape[1] // c
    lane = lax.broadcasted_iota(jnp.int32, (1, LANES), 1)
    state_row = lax.broadcasted_iota(jnp.int32, (LANES, RET_V_DIM), 0)
    contract_rows = (((0,), (0,)), ((), ()))
    contract_lanes = (((1,), (1,)), ((), ()))
    for e in range(2):
        mine = (lane < RET_QK_DIM) if e == 0 else (lane >= RET_QK_DIM)
        my_rows = (state_row < RET_QK_DIM) if e == 0 else (state_row >= RET_QK_DIM)
        vsl = slice(e * RET_V_DIM, (e + 1) * RET_V_DIM)
        chunk_decay = cd_ref[e, 0:1, :]

        def kv_sum(kz_ref, n):
            kv = lax.dot_general(kz_ref[0, n * c:(n + 1) * c, :], v_ref[0, n * c:(n + 1) * c, vsl], contract_rows,
                                 preferred_element_type=F32)
            return jnp.where(my_rows, kv, 0.0)

        state = jnp.zeros((LANES, RET_V_DIM), F32)
        for n in range(nc):
            kvf_ref[e, n] = state
            if n + 1 < nc:
                state = chunk_decay * state + kv_sum(kf_ref, n)
        state = jnp.zeros((LANES, RET_V_DIM), F32)
        for n in range(nc - 1, -1, -1):
            kvb_ref[e, n] = state
            if n > 0:
                state = chunk_decay * state + kv_sum(kb_ref, n)

        for n in range(nc):
            rows = slice(n * c, (n + 1) * c)
            kc = k_ref[0, rows, :]
            vc = v_ref[0, rows, vsl]
            scores = lax.dot_general(q_ref[0, rows, :], jnp.where(mine, kc, jnp.zeros_like(kc)), contract_lanes,
                                     preferred_element_type=F32)
            scores = scores * decay_ref[e]
            o = _dot(scores.astype(BF16), vc)
            o = o + _dot(qf_ref[0, rows, :], kvf_ref[e, n].astype(BF16))
            o = o + _dot(qb_ref[0, rows, :], kvb_ref[e, n].astype(BF16))
            mu = jnp.mean(o, axis=-1, keepdims=True)
            oc = o - mu
            var = jnp.mean(oc * oc, axis=-1, keepdims=True)
            on = oc * lax.rsqrt(var + LN_EPS)
            gate = g_ref[0, rows, vsl].astype(F32)
            o_ref[0, rows, vsl] = (gate * _sigmoid(gate) * on).astype(BF16)


def _retention(q3, k3, v, g, decay, chunk_decay):
    b, seq_len, _ = v.shape
    nc = seq_len // RET_CHUNK
    pairs = RET_HEADS // 2
    qk_spec = lambda form: pl.BlockSpec((1, seq_len, LANES), lambda i, h: (i, 0, form * pairs + h))
    v_spec = pl.BlockSpec((1, seq_len, 2 * RET_V_DIM), lambda i, h: (i, 0, h))
    return pl.pallas_call(
        _retention_kernel,
        grid=(b, pairs),
        in_specs=[qk_spec(0), qk_spec(1), qk_spec(2), qk_spec(0), qk_spec(1), qk_spec(2), v_spec, v_spec,
                  pl.BlockSpec((2, RET_CHUNK, RET_CHUNK), lambda i, h: (h, 0, 0)),
                  pl.BlockSpec((2, SUBLANES, LANES), lambda i, h: (h, 0, 0))],
        out_specs=v_spec,
        out_shape=jax.ShapeDtypeStruct((b, seq_len, RET_V), BF16),
        scratch_shapes=[pltpu.VMEM((2, nc, LANES, RET_V_DIM), F32), pltpu.VMEM((2, nc, LANES, RET_V_DIM), F32)],
        compiler_params=_params("parallel", "parallel"),
        name="retention",
    )(q3, q3, q3, k3, k3, k3, v, g, decay, chunk_decay)


def _s5_matrices(lam_re, lam_im, log_dt, b_re, b_im, c_re, c_im, d_skip):
    hi = lax.Precision.HIGHEST
    cs, n, p = S5_CHUNK, SSM_STATE, SSM_GROUP
    lr = jnp.minimum(lam_re.astype(F32), EIG_CLIP)
    li = lam_im.astype(F32)
    dt = jnp.exp(log_dt.astype(F32))[..., None]
    groups, width = SSM_GROUPS, cs * p
    steps = jnp.arange(cs + 1, dtype=F32)[None, None, :, None]
    mag = jnp.exp((lr * dt)[:, :, None, :] * steps)
    ang = (li * dt)[:, :, None, :] * steps
    pw_re = mag * jnp.cos(ang)
    pw_im = mag * jnp.sin(ang)
    ab_re, ab_im = pw_re[:, :, 1], pw_im[:, :, 1]
    den = lr * lr + li * li
    num_re = ab_re - 1.0
    f_re = ((num_re * lr + ab_im * li) / den)[:, :, None, :]
    f_im = ((ab_im * lr - num_re * li) / den)[:, :, None, :]
    br_t = b_re.astype(F32).transpose(0, 1, 3, 2)
    bi_t = b_im.astype(F32).transpose(0, 1, 3, 2)
    bb_re = f_re * br_t - f_im * bi_t
    bb_im = f_re * bi_t + f_im * br_t
    cr_t = c_re.astype(F32).transpose(0, 1, 3, 2)[:, :, :, None, :]
    ci_t = c_im.astype(F32).transpose(0, 1, 3, 2)[:, :, :, None, :]
    pwn_re = pw_re.transpose(0, 1, 3, 2)[..., None]
    pwn_im = pw_im.transpose(0, 1, 3, 2)[..., None]
    cp_re = cr_t * pwn_re - ci_t * pwn_im
    cp_im = cr_t * pwn_im + ci_t * pwn_re

    lhs = jnp.concatenate([bb_re, -bb_im], axis=-1)
    rhs = jnp.concatenate([cp_re[:, :, :, :cs], cp_im[:, :, :, :cs]], axis=2).reshape(2, groups, 2 * n, width)
    taps = jnp.einsum('dgin,dgnl->dgil', lhs, rhs, precision=hi)
    zeros = jnp.zeros((groups, p, width), F32)
    pad_f = jnp.concatenate([zeros, taps[0]], axis=-1)
    pad_b = jnp.concatenate([taps[1].reshape(groups, p, cs, p)[:, :, ::-1].reshape(groups, p, width), zeros], axis=-1)
    toep = jnp.stack([pad_f[:, :, width - p * s:2 * width - p * s]
                      + pad_b[:, :, p * (cs - 1 - s):p * (cs - 1 - s) + width] for s in range(cs)], axis=1)
    toep = toep.reshape(groups, width, width)

    def state_cols(pw_r, pw_i, bbr, bbi):
        pw_r, pw_i, bbr, bbi = pw_r[:, :, None, :], pw_i[:, :, None, :], bbr[:, None], bbi[:, None]
        shape = (groups, width, n)
        return (pw_r * bbr - pw_i * bbi).reshape(shape), (pw_r * bbi + pw_i * bbr).reshape(shape)

    sf_re, sf_im = state_cols(pw_re[0, :, :cs][:, ::-1], pw_im[0, :, :cs][:, ::-1], bb_re[0], bb_im[0])
    sb_re, sb_im = state_cols(pw_re[1, :, :cs], pw_im[1, :, :cs], bb_re[1], bb_im[1])
    w_state = jnp.concatenate([sf_re, sb_re, sf_im, sb_im], axis=-1)

    rows = lambda a: a.reshape(groups, n, width)
    w_out = jnp.concatenate([rows(cp_re[0, :, :, 1:]), rows(cp_re[1, :, :, 1:][:, :, ::-1]),
                             rows(-cp_im[0, :, :, 1:]), rows(-cp_im[1, :, :, 1:][:, :, ::-1])], axis=1)

    a_re = jnp.concatenate([pw_re[0, :, cs], pw_re[1, :, cs]], axis=-1)
    a_im = jnp.concatenate([pw_im[0, :, cs], pw_im[1, :, cs]], axis=-1)
    a_pow = jnp.stack([a_re, a_im], axis=1)
    d_lanes = jnp.tile(d_skip.astype(F32).reshape(SSM_GROUPS, 1, p), (1, 1, cs))
    return toep.astype(BF16), w_state.astype(BF16), w_out.astype(BF16), a_pow, d_lanes


def _s5_kernel(u_ref, toep_ref, wst_ref, wout_ref, apow_ref, d_ref, z_ref,
               loc_ref, fwd_re_ref, fwd_im_ref, bwd_re_ref, bwd_im_ref):
    ng, nch, tb, width = u_ref.shape
    for gi in range(ng):
        loc_ref[gi] = _dot(u_ref[gi].reshape(nch * tb, width), wst_ref[gi]).reshape(nch, tb, width)

    half = width // 2
    lane = lax.broadcasted_iota(jnp.int32, (1, half), 1)
    is_fwd = lane < SSM_STATE

    def step(i, carry):
        j = nch - 1 - i
        out = []
        for gi in range(ng):
            h_re, h_im = carry[gi]
            a_re = apow_ref[gi, 0:1, :]
            a_im = apow_ref[gi, 1:2, :]
            h_re16, h_im16 = h_re.astype(BF16), h_im.astype(BF16)
            fwd_re_ref[gi, i] = h_re16
            fwd_im_ref[gi, i] = h_im16
            bwd_re_ref[gi, j] = h_re16
            bwd_im_ref[gi, j] = h_im16
            s_re = jnp.where(is_fwd, loc_ref[gi, i, :, :half], loc_ref[gi, j, :, :half])
            s_im = jnp.where(is_fwd, loc_ref[gi, i, :, half:], loc_ref[gi, j, :, half:])
            out.append((a_re * h_re - a_im * h_im + s_re, a_re * h_im + a_im * h_re + s_im))
        return tuple(out)

    zero = jnp.zeros((tb, half), F32)
    lax.fori_loop(0, nch, step, tuple((zero, zero) for _ in range(ng)))

    for gi in range(ng):
        u = u_ref[gi].reshape(nch * tb, width)
        st_re = jnp.where(is_fwd, fwd_re_ref[gi], bwd_re_ref[gi]).reshape(nch * tb, half)
        st_im = jnp.where(is_fwd, fwd_im_ref[gi], bwd_im_ref[gi]).reshape(nch * tb, half)
        state = jnp.concatenate([st_re, st_im], axis=-1)
        y = _dot(u, toep_ref[gi]) + _dot(state, wout_ref[gi]) + u.astype(F32) * d_ref[gi]
        gelu = 0.5 * y * (1.0 + jnp.tanh(math.sqrt(2.0 / math.pi) * (y + 0.044715 * (y * y * y))))
        z_ref[gi] = gelu.astype(BF16).reshape(nch, tb, width)


def _s5(u_g, layer, toep, w_state, w_out, a_pow, d_lanes):
    groups, nch, b, width = u_g.shape
    tb = S5_BATCH_TILE if b % S5_BATCH_TILE == 0 else b
    ng = S5_GROUPS_PER_STEP
    io_spec = pl.BlockSpec((ng, nch, tb, width), lambda g, i: (g, 0, i, 0))
    per_group = lambda rows, cols: pl.BlockSpec((None, ng, rows, cols), lambda g, i: (layer, g, 0, 0))
    half = width // 2
    return pl.pallas_call(
        _s5_kernel,
        grid=(groups // ng, b // tb),
        in_specs=[io_spec, per_group(width, width), per_group(width, width), per_group(width, width),
                  per_group(2, half), per_group(1, width)],
        out_specs=io_spec,
        out_shape=jax.ShapeDtypeStruct(u_g.shape, BF16),
        scratch_shapes=[pltpu.VMEM((ng, nch, tb, width), F32)] + [pltpu.VMEM((ng, nch, tb, half), BF16)] * 4,
        compiler_params=_params("parallel", "parallel"),
        name="s5",
    )(u_g, toep, w_state, w_out, a_pow, d_lanes)


GROUPS_PER_BLOCK = LANES // SSM_GROUP
RUN_STEPS = LANES // SSM_GROUP
RUNS = S5_CHUNK // RUN_STEPS
REGROUP_WIDTH = RUN_STEPS * LANES
REGROUP_TOKENS = 1024
STAGE_PAD = 8


def _selection_matrix():
    row = jnp.arange(REGROUP_WIDTH)
    step, rem = row // LANES, row % LANES
    col = (rem // SSM_GROUP) * LANES + step * SSM_GROUP + rem % SSM_GROUP
    return (col[:, None] == jnp.arange(REGROUP_WIDTH)[None, :]).astype(BF16)


def _to_groups_kernel(u_ref, sel_ref, o_ref, stage_ref, xcat_ref):
    tb, lc, _ = u_ref.shape
    nr = lc // S5_CHUNK
    slab = lc + STAGE_PAD
    for b in range(tb):
        stage_ref[b * slab:b * slab + lc, :] = u_ref[b].astype(F32)
    for r in range(nr):
        for s in range(S5_CHUNK):
            piece = stage_ref[pl.ds(r * S5_CHUNK + s, tb, stride=slab), :]
            run, step = divmod(s, RUN_STEPS)
            xcat_ref[run, r * tb:(r + 1) * tb, step * LANES:(step + 1) * LANES] = piece.astype(BF16)
    for run in range(RUNS):
        grouped = _dot(xcat_ref[run], sel_ref[...]).astype(BF16)
        for gam in range(GROUPS_PER_BLOCK):
            o_ref[gam, :, :, run * LANES:(run + 1) * LANES] = (
                grouped[:, gam * LANES:(gam + 1) * LANES].reshape(nr, tb, LANES))


def _to_tokens_kernel(z_ref, selt_ref, o_ref, stage_ref):
    _, nr, tb, _ = z_ref.shape
    lc = nr * S5_CHUNK
    slab = lc + STAGE_PAD
    for run in range(RUNS):
        zcat = jnp.concatenate([z_ref[gam, :, :, run * LANES:(run + 1) * LANES].reshape(nr * tb, LANES)
                                for gam in range(GROUPS_PER_BLOCK)], axis=-1)
        zt = _dot(zcat, selt_ref[...])
        for r in range(nr):
            for step in range(RUN_STEPS):
                t = run * RUN_STEPS + step
                stage_ref[pl.ds(r * S5_CHUNK + t, tb, stride=slab), :] = (
                    zt[r * tb:(r + 1) * tb, step * LANES:(step + 1) * LANES])
    for b in range(tb):
        o_ref[b] = stage_ref[b * slab:b * slab + lc, :].astype(BF16)


def _regroup_specs(b, seq_len):
    tb = S5_BATCH_TILE if b % S5_BATCH_TILE == 0 else b
    lc = REGROUP_TOKENS
    grid = (b // tb, seq_len // lc, SSM_WIDTH // LANES)
    token_spec = pl.BlockSpec((tb, lc, LANES), lambda i, c, j: (i, c, j))
    group_spec = pl.BlockSpec((GROUPS_PER_BLOCK, lc // S5_CHUNK, tb, S5_CHUNK * SSM_GROUP),
                              lambda i, c, j: (j, c, i, 0))
    stage = pltpu.VMEM((tb * (lc + STAGE_PAD), LANES), F32)
    return tb, grid, token_spec, group_spec, stage


def _to_group_major(u, sel, b, seq_len):
    tb, grid, token_spec, group_spec, stage = _regroup_specs(b, seq_len)
    return pl.pallas_call(
        _to_groups_kernel,
        grid=grid,
        in_specs=[token_spec, _resident(sel.shape)],
        out_specs=group_spec,
        out_shape=jax.ShapeDtypeStruct((SSM_GROUPS, seq_len // S5_CHUNK, b, S5_CHUNK * SSM_GROUP), BF16),
        scratch_shapes=[stage, pltpu.VMEM((RUNS, REGROUP_TOKENS // S5_CHUNK * tb, REGROUP_WIDTH), BF16)],
        compiler_params=_params("parallel", "parallel", "parallel"),
        name="to_groups",
    )(u, sel)


def _to_token_major(z_g, sel_t, b, seq_len):
    tb, grid, token_spec, group_spec, stage = _regroup_specs(b, seq_len)
    return pl.pallas_call(
        _to_tokens_kernel,
        grid=grid,
        in_specs=[group_spec, _resident(sel_t.shape)],
        out_specs=token_spec,
        out_shape=jax.ShapeDtypeStruct((b, seq_len, SSM_WIDTH), BF16),
        scratch_shapes=[stage],
        compiler_params=_params("parallel", "parallel", "parallel"),
        name="to_tokens",
    )(z_g, sel_t)


def _merge_ffn_kernel(x_ref, oret_ref, z_ref, gr_ref, gs_ref, wo_ref, wglu_ref, wout_ref, g2_ref, b2_ref,
                      wgu_ref, wdn_ref, g3_ref, b3_ref, o_ref, x2_ref):
    @pl.when(pl.program_id(0) == 0)
    def _():
        x2_ref[...] = jnp.zeros_like(x2_ref)

    x2 = x2_ref[...]
    o_ref[...] = _layer_norm(ALPHA * x2 + 0.5 * _swiglu(x2, wgu_ref, wdn_ref), g3_ref[...], b3_ref[...])

    z = z_ref[...]
    y_ret = _dot(oret_ref[...], wo_ref[...])
    y_ssm = _dot(z, wglu_ref[:, :D_MODEL]) * _sigmoid(_dot(z, wglu_ref[:, D_MODEL:]))
    merged = gr_ref[...].astype(F32) * y_ret + gs_ref[...].astype(F32) * y_ssm
    mix = _dot(merged.astype(BF16), wout_ref[...])
    x2_ref[...] = _layer_norm(ALPHA * x_ref[...] + mix, g2_ref[...], b2_ref[...])


def _merge_ffn(x, oret, z, gr, gs, layer, w_o, w_glu, w_out, ln2_g, ln2_b, w_gu, w_down, ln3_g, ln3_b, *,
               in_row0=0, rows=None):
    rows = x.shape[0] if rows is None else rows
    grid, in_map, out_spec = _delayed_specs(rows, in_row0, 0, D_MODEL)
    tile = pl.BlockSpec((TOKEN_TILE, D_MODEL), in_map)
    weights = (w_o, w_glu, w_out, ln2_g, ln2_b, w_gu, w_down, ln3_g, ln3_b)
    return pl.pallas_call(
        _merge_ffn_kernel,
        grid=grid,
        in_specs=[tile] * 5 + [_layer(w, layer) for w in weights],
        out_specs=out_spec,
        out_shape=jax.ShapeDtypeStruct((rows, D_MODEL), F32),
        scratch_shapes=[pltpu.VMEM((TOKEN_TILE, D_MODEL), F32)],
        compiler_params=_params("arbitrary"),
        name="merge_ffn",
    )(x, oret, z, gr, gs, *weights)


def _trunk(xs, params):
    (ffn1_w_gu, ffn1_w_down, ln1_g, ln1_b, w_in, b_gate, ret_w_o, s5_lam_re, s5_lam_im, s5_log_dt,
     s5_b_re, s5_b_im, s5_c_re, s5_c_im, s5_d, s5_w_glu, w_out, ln2_g, ln2_b,
     ffn2_w_gu, ffn2_w_down, ln3_g, ln3_b) = params
    seq_len = xs[0].shape[1]
    assert all(x.shape[1:] == (seq_len, D_MODEL) for x in xs)
    assert seq_len % TOKEN_TILE == 0 and seq_len % RET_CHUNK == 0 and seq_len % REGROUP_TOKENS == 0
    sizes = [x.shape[0] * seq_len for x in xs]
    starts = [sum(sizes[:i]) for i in range(len(xs))]
    b, t = sum(x.shape[0] for x in xs), sum(sizes)
    cos_t, sin_t = _rotary_tables(seq_len)
    decay, edge_tok, chunk_decay = _retention_tables()
    sel = _selection_matrix()
    sel_t = sel.T
    s5_mats = jax.vmap(_s5_matrices)(s5_lam_re, s5_lam_im, s5_log_dt, s5_b_re, s5_b_im, s5_c_re, s5_c_im, s5_d)
    bf = lambda w: w.astype(BF16)
    row = lambda v: v.reshape(DEPTH, 1, v.shape[-1])
    ffn1 = (bf(ffn1_w_gu), bf(ffn1_w_down), row(ln1_g), row(ln1_b))
    w_in, b_gate = bf(w_in), row(b_gate)
    tail_weights = (bf(ret_w_o), bf(s5_w_glu), bf(w_out), row(ln2_g), row(ln2_b),
                    bf(ffn2_w_gu), bf(ffn2_w_down), row(ln3_g), row(ln3_b))
    x = None
    for l in range(DEPTH):
        if l == 0:
            for xi, size, start in zip(xs, sizes, starts):
                x = _ffn_ln(xi.reshape(size, D_MODEL), l, *ffn1, out_rows=t, out_row0=start, into=x)
        else:
            x = _ffn_ln(x, l, *ffn1)
        q, k, v, g, u, gr, gs = _in_proj(x, l, w_in, b_gate, cos_t, sin_t, edge_tok, seq_len)
        shape3 = lambda a: a.reshape(b, seq_len, a.shape[-1])
        oret = _retention(shape3(q), shape3(k), shape3(v), shape3(g), decay, chunk_decay)
        z = _to_token_major(_s5(_to_group_major(shape3(u), sel, b, seq_len), l, *s5_mats), sel_t, b, seq_len)
        tail = (x, oret.reshape(t, RET_V), z.reshape(t, SSM_WIDTH), gr, gs, l) + tail_weights
        if l < DEPTH - 1:
            x = _merge_ffn(*tail)
        else:
            ys = [_merge_ffn(*tail, in_row0=start, rows=size) for size, start in zip(sizes, starts)]
    return [y.reshape(xi.shape) for y, xi in zip(ys, xs)]


def kernel(x_prompt, x_sample, ffn1_w_gu, ffn1_w_down, ln1_g, ln1_b, w_in, b_gate, ret_w_o, s5_lam_re, s5_lam_im, s5_log_dt, s5_b_re, s5_b_im, s5_c_re, s5_c_im, s5_d, s5_w_glu, w_out, ln2_g, ln2_b, ffn2_w_gu, ffn2_w_down, ln3_g, ln3_b):
    params = (ffn1_w_gu, ffn1_w_down, ln1_g, ln1_b, w_in, b_gate, ret_w_o, s5_lam_re, s5_lam_im, s5_log_dt,
              s5_b_re, s5_b_im, s5_c_re, s5_c_im, s5_d, s5_w_glu, w_out, ln2_g, ln2_b,
              ffn2_w_gu, ffn2_w_down, ln3_g, ln3_b)
    y_prompt, y_sample = _trunk([x_prompt, x_sample], params)
    return (y_prompt, y_sample)
```
